```python
import math
import jax, jax.numpy as jnp
from jax import lax
import numpy as np

D_MODEL = 1024
BATCH = 16
SEQ = 4096
DEPTH = 4

CHUNK = 64
EPS = 1e-6
DA_HEADS = 8
DA_HEAD_DIM = 64
DA_V_DIM = 2 * DA_HEAD_DIM
DA_WIDTH = DA_HEADS * DA_V_DIM
QK_COLS = DA_HEADS * 2 * DA_HEAD_DIM
Q_BLOCK = 128
REL_BUCKETS = 32
REL_MAX_DIST = 128
SG_BLOCK = 128
SG_GROUPS = 4
SG_WIDTH = D_MODEL
SG_GROUP_DIM = SG_WIDTH // SG_GROUPS
LRU_WIDTH = D_MODEL
LRU_BLOCKS = 8
LRU_BLOCK_DIM = LRU_WIDTH // LRU_BLOCKS
CONV_WIDTH = 4
LRU_C = 8.0
N_BRANCH = 3
BRANCH_WIDTH = D_MODEL
D_FF = 4 * D_MODEL
IN_COLS = 2 * QK_COLS + DA_WIDTH + 2 * SG_WIDTH + 2 * LRU_WIDTH + N_BRANCH * D_MODEL

kernel_name = "hybrid_diffattn_gmlp_rglru_gated_trunk"


def rmsnorm(x, g):
    xf = x.astype(jnp.float32)
    y = xf * lax.rsqrt(jnp.mean(xf * xf, axis=-1, keepdims=True) + EPS)
    return (y * g.astype(jnp.float32)).astype(x.dtype)


def layernorm(x, g, b):
    xf = x.astype(jnp.float32)
    mu = jnp.mean(xf, axis=-1, keepdims=True)
    var = jnp.mean(jnp.square(xf - mu), axis=-1, keepdims=True)
    y = (xf - mu) * lax.rsqrt(var + EPS)
    return (y * g.astype(jnp.float32) + b.astype(jnp.float32)).astype(x.dtype)


def t5_bucket(rel):
    nb = REL_BUCKETS // 2
    max_exact = nb // 2
    ret = (rel > 0).astype(jnp.int32) * nb
    n = jnp.abs(rel)
    nf = jnp.maximum(n, 1).astype(jnp.float32)
    large = max_exact + (jnp.log(nf / max_exact) / math.log(REL_MAX_DIST / max_exact)
                         * (nb - max_exact)).astype(jnp.int32)
    large = jnp.minimum(large, nb - 1)
    return ret + jnp.where(n < max_exact, n, large)


def diff_attention(q, k, v, lam_params, subln_g, rel_table, layer_idx):
    B, S = q.shape[0], q.shape[1]
    nblk = S // Q_BLOCK
    scale = DA_HEAD_DIM ** -0.5
    lam_init = 0.8 - 0.6 * math.exp(-0.3 * layer_idx)
    lp = lam_params.astype(jnp.float32)
    lam = jnp.exp(jnp.sum(lp[0] * lp[1])) - jnp.exp(jnp.sum(lp[2] * lp[3])) + lam_init
    key_pos = jnp.arange(S, dtype=jnp.int32)
    qb = q.reshape(B, nblk, Q_BLOCK, DA_HEADS, 2, DA_HEAD_DIM).transpose(1, 0, 2, 3, 4, 5)

    def one_block(args):
        q_blk, blk = args
        q_pos = blk * Q_BLOCK + jnp.arange(Q_BLOCK, dtype=jnp.int32)
        rel = key_pos[None, :] - q_pos[:, None]
        bias = rel_table[t5_bucket(rel)].transpose(2, 0, 1).astype(jnp.float32)
        allowed = (key_pos[None, :] // CHUNK) <= (q_pos[:, None] // CHUNK)
        logits = jnp.einsum('bqhmd,bkhmd->bhmqk', q_blk, k).astype(jnp.float32) * scale
        logits = logits + bias[None, :, None]
        logits = jnp.where(allowed, logits, jnp.finfo(jnp.float32).min)
        p = jax.nn.softmax(logits, axis=-1)
        w = p[:, :, 0] - lam * p[:, :, 1]
        return jnp.einsum('bhqk,bkhe->bqhe', w.astype(v.dtype), v)

    out = lax.map(one_block, (qb, jnp.arange(nblk, dtype=jnp.int32)))
    out = out.transpose(1, 0, 2, 3, 4).reshape(B, S, DA_HEADS, DA_V_DIM)
    out = rmsnorm(out, subln_g) * (1.0 - lam_init)
    return out.reshape(B, S, DA_WIDTH)


def spatial_gating(uv, ln_g, ln_b, w_s, b_s):
    B, S = uv.shape[0], uv.shape[1]
    u, v = jnp.split(jax.nn.gelu(uv), 2, axis=-1)
    v = layernorm(v, ln_g, ln_b)
    vb = v.reshape(B, S // SG_BLOCK, SG_BLOCK, SG_GROUPS, SG_GROUP_DIM)
    pos = jnp.arange(SG_BLOCK)
    mask = (pos[None, :] // CHUNK) <= (pos[:, None] // CHUNK)
    w = jnp.where(mask[None], w_s, jnp.zeros_like(w_s))
    mixed = jnp.einsum('gqk,bnkgc->bnqgc', w, vb) + b_s.T[None, None, :, :, None]
    return u * mixed.reshape(B, S, SG_WIDTH)


def rglru_branch(xr, gate, conv_w, conv_b, w_a, b_a, w_i, b_i, lam):
    B, S, C = xr.shape
    xc = lax.conv_general_dilated(xr, conv_w[:, None, :], window_strides=(1,),
                                  padding=[(CONV_WIDTH - 1, 0)],
                                  dimension_numbers=('NWC', 'WIO', 'NWC'),
                                  feature_group_count=C) + conv_b
    xb = xc.reshape(B, S, LRU_BLOCKS, LRU_BLOCK_DIM)
    r = jax.nn.sigmoid(jnp.einsum('bshi,hij->bshj', xb, w_a).reshape(B, S, C) + b_a)
    i = jax.nn.sigmoid(jnp.einsum('bshi,hij->bshj', xb, w_i).reshape(B, S, C) + b_i)
    log_a = -LRU_C * r.astype(jnp.float32) * jax.nn.softplus(-lam.astype(jnp.float32))
    a = jnp.exp(log_a)
    mult = jnp.sqrt(jnp.maximum(-jnp.expm1(2.0 * log_a), 0.0))
    u = xc.astype(jnp.float32) * i.astype(jnp.float32) * mult

    def combine(c1, c2):
        a1, b1 = c1
        a2, b2 = c2
        return a1 * a2, a2 * b1 + b2

    _, h = lax.associative_scan(combine, (a, u), axis=1)
    return h.astype(xr.dtype) * jax.nn.gelu(gate)


def setup_inputs(seed: int = 0) -> dict:
    key = jax.random.key(seed)
    ks = jax.random.split(key, 26)

    def nrm(k, shape, scale):
        return jax.random.normal(k, shape, jnp.float32) * scale

    u = jax.random.uniform(ks[16], (DEPTH, LRU_WIDTH), jnp.float32, minval=0.9, maxval=0.999)
    a = u ** (1.0 / LRU_C)
    return {
        "x": nrm(ks[0], (BATCH, SEQ, D_MODEL), 1.0),
        "w_in": nrm(ks[1], (DEPTH, D_MODEL, IN_COLS), D_MODEL ** -0.5),
        "g_mix": 1.0 + nrm(ks[2], (DEPTH, D_MODEL), 0.02),
        "da_lambda": nrm(ks[3], (DEPTH, 4, DA_HEAD_DIM), 0.1),
        "da_subln_g": 1.0 + nrm(ks[4], (DEPTH, DA_V_DIM), 0.02),
        "rel_bias": nrm(ks[5], (REL_BUCKETS, DA_HEADS), 0.2),
        "sg_ln_g": 1.0 + nrm(ks[6], (DEPTH, SG_WIDTH), 0.02),
        "sg_ln_b": nrm(ks[7], (DEPTH, SG_WIDTH), 0.02),
        "sg_w": nrm(ks[8], (DEPTH, SG_GROUPS, SG_BLOCK, SG_BLOCK), SG_BLOCK ** -0.5),
        "sg_b": 1.0 + nrm(ks[9], (DEPTH, SG_GROUPS, SG_BLOCK), 0.02),
        "lru_conv_w": nrm(ks[10], (DEPTH, CONV_WIDTH, LRU_WIDTH), CONV_WIDTH ** -0.5),
        "lru_conv_b": nrm(ks[11], (DEPTH, LRU_WIDTH), 0.02),
        "lru_w_a": nrm(ks[12], (DEPTH, LRU_BLOCKS, LRU_BLOCK_DIM, LRU_BLOCK_DIM), LRU_BLOCK_DIM ** -0.5),
        "lru_b_a": nrm(ks[13], (DEPTH, LRU_WIDTH), 0.02),
        "lru_w_i": nrm(ks[14], (DEPTH, LRU_BLOCKS, LRU_BLOCK_DIM, LRU_BLOCK_DIM), LRU_BLOCK_DIM ** -0.5),
        "lru_b_i": nrm(ks[15], (DEPTH, LRU_WIDTH), 0.02),
        "lru_lambda": jnp.log(a) - jnp.log1p(-a),
        "b_gate": nrm(ks[17], (DEPTH, N_BRANCH, D_MODEL), 0.02),
        "w_branch": nrm(ks[18], (DEPTH, N_BRANCH, BRANCH_WIDTH, D_MODEL), BRANCH_WIDTH ** -0.5),
        "w_out": nrm(ks[19], (DEPTH, D_MODEL, D_MODEL), D_MODEL ** -0.5),
        "g_mlp": 1.0 + nrm(ks[20], (DEPTH, D_MODEL), 0.02),
        "w_up": nrm(ks[21], (DEPTH, D_MODEL, D_FF), D_MODEL ** -0.5),
        "w_down": nrm(ks[22], (DEPTH, D_FF, D_MODEL), D_FF ** -0.5),
        "g_final": 1.0 + nrm(ks[23], (D_MODEL,), 0.02),
    }


def reference(x, w_in, g_mix, da_lambda, da_subln_g, rel_bias, sg_ln_g, sg_ln_b, sg_w, sg_b,
              lru_conv_w, lru_conv_b, lru_w_a, lru_b_a, lru_w_i, lru_b_i, lru_lambda,
              b_gate, w_branch, w_out, g_mlp, w_up, w_down, g_final):
    B, S = x.shape[0], x.shape[1]
    sizes = (QK_COLS, QK_COLS, DA_WIDTH, 2 * SG_WIDTH, LRU_WIDTH, LRU_WIDTH, N_BRANCH * D_MODEL)
    split_points = []
    acc = 0
    for s in sizes[:-1]:
        acc += s
        split_points.append(acc)

    for l in range(DEPTH):
        h = rmsnorm(x, g_mix[l])
        proj = jnp.einsum('bsd,dc->bsc', h, w_in[l])
        q, k, v, uv, xr, xg, gl = jnp.split(proj, split_points, axis=-1)
        q = q.reshape(B, S, DA_HEADS, 2, DA_HEAD_DIM)
        k = k.reshape(B, S, DA_HEADS, 2, DA_HEAD_DIM)
        v = v.reshape(B, S, DA_HEADS, DA_V_DIM)
        y_a = diff_attention(q, k, v, da_lambda[l], da_subln_g[l], rel_bias, l)
        y_b = spatial_gating(uv, sg_ln_g[l], sg_ln_b[l], sg_w[l], sg_b[l])
        y_c = rglru_branch(xr, xg, lru_conv_w[l], lru_conv_b[l], lru_w_a[l], lru_b_a[l],
                           lru_w_i[l], lru_b_i[l], lru_lambda[l])
        gates = jax.nn.sigmoid(gl.reshape(B, S, N_BRANCH, D_MODEL) + b_gate[l])
        merged = (gates[:, :, 0] * jnp.einsum('bsc,cd->bsd', y_a, w_branch[l, 0])
                  + gates[:, :, 1] * jnp.einsum('bsc,cd->bsd', y_b, w_branch[l, 1])
                  + gates[:, :, 2] * jnp.einsum('bsc,cd->bsd', y_c, w_branch[l, 2]))
        x = x + jnp.einsum('bsd,de->bse', merged, w_out[l])
        h = rmsnorm(x, g_mlp[l])
        hid = jnp.square(jax.nn.relu(jnp.einsum('bsd,df->bsf', h, w_up[l])))
        x = x + jnp.einsum('bsf,fd->bsd', hid, w_down[l])
    return rmsnorm(x, g_final)
```

```python
import functools
import math

import jax
import jax.numpy as jnp
from jax import lax
from jax.experimental import pallas as pl
from jax.experimental.pallas import tpu as pltpu

EPS = 1e-6
CHUNK = 64
DA_HEADS = 8
DA_HEAD_DIM = 64
DA_V_DIM = 2 * DA_HEAD_DIM
REL_BUCKETS = 32
REL_MAX_DIST = 128
SG_BLOCK = 128
SG_GROUPS = 4
LRU_BLOCKS = 8
CONV_WIDTH = 4
LRU_C = 8.0
N_BRANCH = 3

LANES = 128
SUBLANES = 8
VMEM_LIMIT = 56 * 1024 * 1024
NEG = -1e30

ATT_TILE = 256
BF16 = jnp.bfloat16
F32 = jnp.float32


def _params(*sem):
    return pltpu.CompilerParams(dimension_semantics=sem, vmem_limit_bytes=VMEM_LIMIT)


def _rms(x, g):
    ms = jnp.mean(x * x, axis=-1, keepdims=True)
    return x * lax.rsqrt(ms + EPS) * g


def _inproj_kernel(x_ref, g_ref, w_ref, o_ref, h_ref):
    @pl.when(pl.program_id(1) == 0)
    def _():
        h_ref[...] = _rms(x_ref[...], g_ref[...]).astype(BF16)

    o_ref[...] = jnp.dot(h_ref[...], w_ref[...], preferred_element_type=F32).astype(o_ref.dtype)


def _inproj(x2, g, w):
    T, D = x2.shape
    N = w.shape[1]
    tm = min(1024, T)
    tn = 1024
    return pl.pallas_call(
        _inproj_kernel,
        grid=(T // tm, N // tn),
        in_specs=[
            pl.BlockSpec((tm, D), lambda i, j: (i, 0)),
            pl.BlockSpec((1, D), lambda i, j: (0, 0)),
            pl.BlockSpec((D, tn), lambda i, j: (0, j)),
        ],
        out_specs=pl.BlockSpec((tm, tn), lambda i, j: (i, j)),
        out_shape=jax.ShapeDtypeStruct((T, N), BF16),
        scratch_shapes=[pltpu.VMEM((tm, D), BF16)],
        compiler_params=_params("parallel", "arbitrary"),
        name="inproj",
    )(x2, g.reshape(1, D), w)


def _attn_kernel(cfar_ref, lp_ref, q_ref, k_ref, v_ref, bias_ref, g_ref, o_ref,
                 k1_ref, k2_ref, m1_ref, l1_ref, a1_ref, m2_ref, l2_ref, a2_ref,
                 *, seq, tile, lam_init):
    h = pl.program_id(0)
    nq = seq // tile
    lane = lax.broadcasted_iota(jnp.int32, (1, LANES), 1)
    lo_half = lane < DA_HEAD_DIM

    c = jnp.full((1, LANES), cfar_ref[h], F32)
    c_hi = c.astype(BF16).astype(F32)
    c_lo = c - c_hi
    kfill1 = jnp.where(lane == DA_HEAD_DIM, c_hi, jnp.where(lane == DA_HEAD_DIM + 1, c_lo, 0.0))
    kfill2 = jnp.where(lane == 0, c_hi, jnp.where(lane == 1, c_lo, 0.0))
    qfill1 = jnp.where((lane == DA_HEAD_DIM) | (lane == DA_HEAD_DIM + 1), 1.0, 0.0)
    qfill2 = jnp.where((lane == 0) | (lane == 1), 1.0, 0.0)

    kk = k_ref[...].astype(F32)
    k1_ref[...] = jnp.where(lo_half, kk, kfill1).astype(BF16)
    k2_ref[...] = jnp.where(lo_half, kfill2, kk).astype(BF16)

    lp = lp_ref[...]
    lam = (jnp.exp(jnp.sum(lp[0:1] * lp[1:2], axis=-1, keepdims=True))
           - jnp.exp(jnp.sum(lp[2:3] * lp[3:4], axis=-1, keepdims=True)) + lam_init)

    def update(qa, ka, v, bias, m_ref, l_ref, a_ref):
        s = lax.dot_general(qa, ka, (((1,), (1,)), ((), ())), preferred_element_type=F32)
        if bias is not None:
            s = s + bias
        m_prev = m_ref[...]
        m_new = jnp.maximum(m_prev, jnp.max(s, axis=-1, keepdims=True))
        alpha = jnp.exp(m_prev - m_new)
        p = jnp.exp(s - m_new)
        l_ref[...] = alpha * l_ref[...] + jnp.sum(p, axis=-1, keepdims=True)
        a_ref[...] = alpha * a_ref[...] + jnp.dot(p.astype(BF16), v, preferred_element_type=F32)
        m_ref[...] = m_new

    def q_body(i, carry):
        q0 = pl.multiple_of(i * tile, tile)
        qq = q_ref[pl.ds(q0, tile), :].astype(F32)
        qa1 = jnp.where(lo_half, qq, qfill1).astype(BF16)
        qa2 = jnp.where(lo_half, qfill2, qq).astype(BF16)
        for m_ref, l_ref, a_ref in ((m1_ref, l1_ref, a1_ref), (m2_ref, l2_ref, a2_ref)):
            m_ref[...] = jnp.full(m_ref.shape, NEG, F32)
            l_ref[...] = jnp.zeros(l_ref.shape, F32)
            a_ref[...] = jnp.zeros(a_ref.shape, F32)

        def kv_tile(j, bias):
            k0 = pl.multiple_of(j * tile, tile)
            v = v_ref[pl.ds(k0, tile), :]
            update(qa1, k1_ref[pl.ds(k0, tile), :], v, bias, m1_ref, l1_ref, a1_ref)
            update(qa2, k2_ref[pl.ds(k0, tile), :], v, bias, m2_ref, l2_ref, a2_ref)

        def far_body(j, c2):
            kv_tile(j, None)
            return c2

        lax.fori_loop(0, jnp.maximum(i - 1, 0), far_body, 0)

        @pl.when(i >= 1)
        def _():
            kv_tile(i - 1, bias_ref[1])

        kv_tile(i, bias_ref[0])

        o = a1_ref[...] / l1_ref[...] - lam * (a2_ref[...] / l2_ref[...])
        y = _rms(o, g_ref[...]) * (1.0 - lam_init)
        o_ref[pl.ds(q0, tile), :] = y.astype(o_ref.dtype)
        return carry

    lax.fori_loop(0, nq, q_body, 0)


def _t5_bucket(rel):
    nb = REL_BUCKETS // 2
    max_exact = nb // 2
    ret = (rel > 0).astype(jnp.int32) * nb
    n = jnp.abs(rel)
    nf = jnp.maximum(n, 1).astype(jnp.float32)
    large = max_exact + (jnp.log(nf / max_exact) / math.log(REL_MAX_DIST / max_exact)
                         * (nb - max_exact)).astype(jnp.int32)
    large = jnp.minimum(large, nb - 1)
    return ret + jnp.where(n < max_exact, n, large)


def _bias_tiles(rel_table, tile):
    pos = jnp.arange(tile, dtype=jnp.int32)
    cfar = rel_table[_t5_bucket(jnp.int32(-(tile + 1)))].astype(F32)
    tiles = []
    for d in (0, 1):
        rel = (pos[None, :] - d * tile) - pos[:, None]
        b = rel_table[_t5_bucket(rel)].astype(F32).transpose(2, 0, 1) - cfar[:, None, None]
        if d == 0:
            allowed = (pos[None, :] // CHUNK) <= (pos[:, None] // CHUNK)
            b = jnp.where(allowed[None], b, NEG)
        tiles.append(b)
    return jnp.stack(tiles, axis=1), cfar


def _attention(proj, B, S, lam_params, subln_g, bias_tiles, cfar, layer_idx):
    T = B * S
    tile = ATT_TILE
    assert tile + 1 >= REL_MAX_DIST and S % tile == 0
    lam_init = 0.8 - 0.6 * math.exp(-0.3 * layer_idx)
    kern = functools.partial(_attn_kernel, seq=S, tile=tile, lam_init=lam_init)
    H = DA_HEADS
    return pl.pallas_call(
        kern,
        grid=(H, B),
        in_specs=[
            pl.BlockSpec(memory_space=pltpu.SMEM),
            pl.BlockSpec((4, DA_HEAD_DIM), lambda h, b: (0, 0)),
            pl.BlockSpec((S, LANES), lambda h, b: (b, h)),
            pl.BlockSpec((S, LANES), lambda h, b: (b, H + h)),
            pl.BlockSpec((S, LANES), lambda h, b: (b, 2 * H + h)),
            pl.BlockSpec((None, 2, tile, tile), lambda h, b: (h, 0, 0, 0)),
            pl.BlockSpec((1, DA_V_DIM), lambda h, b: (0, 0)),
        ],
        out_specs=pl.BlockSpec((S, LANES), lambda h, b: (b, h)),
        out_shape=jax.ShapeDtypeStruct((T, H * DA_V_DIM), BF16),
        scratch_shapes=[
            pltpu.VMEM((S, LANES), BF16), pltpu.VMEM((S, LANES), BF16),
            pltpu.VMEM((tile, 1), F32), pltpu.VMEM((tile, 1), F32), pltpu.VMEM((tile, DA_V_DIM), F32),
            pltpu.VMEM((tile, 1), F32), pltpu.VMEM((tile, 1), F32), pltpu.VMEM((tile, DA_V_DIM), F32),
        ],
        compiler_params=_params("parallel", "parallel"),
        name="diff_attn",
    )(cfar, lam_params.astype(F32), proj, proj, proj, bias_tiles, subln_g.reshape(1, DA_V_DIM).astype(F32))


def _sgu_kernel(u_ref, v_ref, g_ref, b_ref, w_ref, bs_ref, o_ref, *, rows):
    v = jax.nn.gelu(v_ref[...].astype(F32))
    mu = jnp.mean(v, axis=-1, keepdims=True)
    vc = v - mu
    var = jnp.mean(vc * vc, axis=-1, keepdims=True)
    vn = (vc * lax.rsqrt(var + EPS) * g_ref[...] + b_ref[...]).astype(BF16)
    pos_q = lax.broadcasted_iota(jnp.int32, (SG_BLOCK, SG_BLOCK), 0)
    pos_k = lax.broadcasted_iota(jnp.int32, (SG_BLOCK, SG_BLOCK), 1)
    mask = (pos_k // CHUNK) <= (pos_q // CHUNK)
    gd = v.shape[1] // SG_GROUPS
    for g in range(SG_GROUPS):
        w = jnp.where(mask, w_ref[g], 0.0).astype(BF16)
        cs = slice(g * gd, (g + 1) * gd)
        for n in range(rows // SG_BLOCK):
            rs = slice(n * SG_BLOCK, (n + 1) * SG_BLOCK)
            mixed = jnp.dot(w, vn[rs, cs], preferred_element_type=F32) + bs_ref[:, cs]
            u = jax.nn.gelu(u_ref[rs, cs].astype(F32))
            o_ref[rs, cs] = (u * mixed).astype(o_ref.dtype)


def _sgu(proj, T, ln_g, ln_b, w_s, b_s, col_u, col_v):
    W = ln_g.shape[0]
    tm = min(512, T)
    gd = W // SG_GROUPS
    bsb = jnp.repeat(b_s.T.astype(F32), gd, axis=1)
    kern = functools.partial(_sgu_kernel, rows=tm)
    return pl.pallas_call(
        kern,
        grid=(T // tm,),
        in_specs=[
            pl.BlockSpec((tm, W), lambda i: (i, col_u)),
            pl.BlockSpec((tm, W), lambda i: (i, col_v)),
            pl.BlockSpec((1, W), lambda i: (0, 0)),
            pl.BlockSpec((1, W), lambda i: (0, 0)),
            pl.BlockSpec((SG_GROUPS, SG_BLOCK, SG_BLOCK), lambda i: (0, 0, 0)),
            pl.BlockSpec((SG_BLOCK, W), lambda i: (0, 0)),
        ],
        out_specs=pl.BlockSpec((tm, W), lambda i: (i, 0)),
        out_shape=jax.ShapeDtypeStruct((T, W), BF16),
        compiler_params=_params("parallel"),
        name="sgu",
    )(proj, proj, ln_g.reshape(1, W).astype(F32), ln_b.reshape(1, W).astype(F32),
      w_s.astype(F32), bsb)


def _lru_kernel(xr_ref, prev_ref, xg_ref, cw_ref, cb_ref, wa_ref, ba_ref, wi_ref, bi_ref, lam_ref,
                o_ref, ext_ref, a_ref, b_ref, h_ref, *, rows):
    t = pl.program_id(1)
    C = xr_ref.shape[1]
    halo = SUBLANES

    @pl.when(t == 0)
    def _():
        h_ref[...] = jnp.zeros(h_ref.shape, F32)

    prev = prev_ref[...].astype(F32)
    ext_ref[0:halo, :] = jnp.where(t == 0, jnp.zeros_like(prev), prev)
    ext_ref[halo:halo + rows, :] = xr_ref[...].astype(F32)

    xc = jnp.zeros((rows, C), F32) + cb_ref[...]
    for j in range(CONV_WIDTH):
        off = halo - (CONV_WIDTH - 1) + j
        xc = xc + cw_ref[j:j + 1, :] * ext_ref[off:off + rows, :]

    z = lam_ref[...]
    nz = -z
    softplus_neg = jnp.maximum(nz, 0.0) + jnp.log1p(jnp.exp(-jnp.abs(nz)))
    xcb = xc.astype(BF16)
    bd = C // LRU_BLOCKS
    r_parts, i_parts = [], []
    for hb in range(LRU_BLOCKS):
        cs = slice(hb * bd, (hb + 1) * bd)
        r_parts.append(jnp.dot(xcb[:, cs], wa_ref[hb], preferred_element_type=F32))
        i_parts.append(jnp.dot(xcb[:, cs], wi_ref[hb], preferred_element_type=F32))
    r = jax.nn.sigmoid(jnp.concatenate(r_parts, axis=1) + ba_ref[...])
    ig = jax.nn.sigmoid(jnp.concatenate(i_parts, axis=1) + bi_ref[...])
    log_a = -LRU_C * r * softplus_neg
    a = jnp.exp(log_a)
    mult = jnp.sqrt(jnp.maximum(-jnp.tanh(log_a) * (a * a + 1.0), 0.0))
    u = xc * ig * mult

    a3 = a.reshape(rows // SUBLANES, SUBLANES, C)
    b3 = u.reshape(rows // SUBLANES, SUBLANES, C)
    row = lax.broadcasted_iota(jnp.int32, (1, SUBLANES, C), 1)
    s = 1
    while s < SUBLANES:
        keep = row >= s
        a_sh = pltpu.roll(a3, s, axis=1)
        b_sh = pltpu.roll(b3, s, axis=1)
        b3 = jnp.where(keep, b3 + a3 * b_sh, b3)
        a3 = jnp.where(keep, a3 * a_sh, a3)
        s *= 2
    a_ref[...] = a3.reshape(rows, C)
    b_ref[...] = b3.reshape(rows, C)

    def grp(i, hprev):
        r0 = pl.multiple_of(i * SUBLANES, SUBLANES)
        hh = b_ref[pl.ds(r0, SUBLANES), :] + a_ref[pl.ds(r0, SUBLANES), :] * hprev
        gate = jax.nn.gelu(xg_ref[pl.ds(r0, SUBLANES), :].astype(F32))
        o_ref[pl.ds(r0, SUBLANES), :] = (hh * gate).astype(o_ref.dtype)
        return jnp.broadcast_to(hh[SUBLANES - 1:SUBLANES, :], (SUBLANES, C))

    hlast = lax.fori_loop(0, rows // SUBLANES, grp, jnp.broadcast_to(h_ref[...], (SUBLANES, C)))
    h_ref[...] = hlast[0:1, :]


def _lru(proj, B, S, conv_w, conv_b, w_a, b_a, w_i, b_i, lam, col_r, col_g):
    T = B * S
    C = lam.shape[0]
    ts = min(512, S)
    nt = S // ts
    hb = ts // SUBLANES
    kern = functools.partial(_lru_kernel, rows=ts)
    vec = lambda a: a.reshape(1, C).astype(F32)
    const2 = lambda b, t: (0, 0)
    return pl.pallas_call(
        kern,
        grid=(B, nt),
        in_specs=[
            pl.BlockSpec((ts, C), lambda b, t: (b * nt + t, col_r)),
            pl.BlockSpec((SUBLANES, C),
                         lambda b, t: (jnp.maximum((b * nt + t) * hb - 1, 0), col_r)),
            pl.BlockSpec((ts, C), lambda b, t: (b * nt + t, col_g)),
            pl.BlockSpec((CONV_WIDTH, C), const2),
            pl.BlockSpec((1, C), const2),
            pl.BlockSpec((LRU_BLOCKS, C // LRU_BLOCKS, C // LRU_BLOCKS), lambda b, t: (0, 0, 0)),
            pl.BlockSpec((1, C), const2),
            pl.BlockSpec((LRU_BLOCKS, C // LRU_BLOCKS, C // LRU_BLOCKS), lambda b, t: (0, 0, 0)),
            pl.BlockSpec((1, C), const2),
            pl.BlockSpec((1, C), const2),
        ],
        out_specs=pl.BlockSpec((ts, C), lambda b, t: (b * nt + t, 0)),
        out_shape=jax.ShapeDtypeStruct((T, C), BF16),
        scratch_shapes=[
            pltpu.VMEM((ts + SUBLANES, C), F32),
            pltpu.VMEM((ts, C), F32),
            pltpu.VMEM((ts, C), F32),
            pltpu.VMEM((1, C), F32),
        ],
        compiler_params=_params("parallel", "arbitrary"),
        name="rglru",
    )(proj, proj, proj, conv_w.astype(F32), vec(conv_b), w_a.astype(BF16), vec(b_a),
      w_i.astype(BF16), vec(b_i), vec(lam))


def _merge_kernel(ya_ref, yb_ref, yc_ref, g0_ref, g1_ref, g2_ref, bg_ref, wb_ref, wo_ref, x_ref, o_ref):
    merged = None
    for k, (y_ref, g_ref) in enumerate(((ya_ref, g0_ref), (yb_ref, g1_ref), (yc_ref, g2_ref))):
        gate = jax.nn.sigmoid(g_ref[...].astype(F32) + bg_ref[k:k + 1, :])
        term = gate * jnp.dot(y_ref[...], wb_ref[k], preferred_element_type=F32)
        merged = term if merged is None else merged + term
    o_ref[...] = x_ref[...] + jnp.dot(merged.astype(BF16), wo_ref[...], preferred_element_type=F32)


def _merge(ya, yb, yc, proj, col_gate, b_gate, w_branch, w_out, x2):
    T, D = x2.shape
    tm = min(512, T)
    tok = lambda i: (i, 0)
    return pl.pallas_call(
        _merge_kernel,
        grid=(T // tm,),
        in_specs=[
            pl.BlockSpec((tm, D), tok), pl.BlockSpec((tm, D), tok), pl.BlockSpec((tm, D), tok),
            pl.BlockSpec((tm, D), lambda i: (i, col_gate)),
            pl.BlockSpec((tm, D), lambda i: (i, col_gate + 1)),
            pl.BlockSpec((tm, D), lambda i: (i, col_gate + 2)),
            pl.BlockSpec((N_BRANCH, D), lambda i: (0, 0)),
            pl.BlockSpec((N_BRANCH, D, D), lambda i: (0, 0, 0)),
            pl.BlockSpec((D, D), lambda i: (0, 0)),
            pl.BlockSpec((tm, D), tok),
        ],
        out_specs=pl.BlockSpec((tm, D), tok),
        out_shape=jax.ShapeDtypeStruct((T, D), F32),
        compiler_params=_params("parallel"),
        name="merge",
    )(ya, yb, yc, proj, proj, proj, b_gate.astype(F32), w_branch, w_out, x2)


def _mlp_kernel(x_ref, g_ref, wu_ref, wd_ref, gf_ref, o_ref, *, ft, final_norm):
    x = x_ref[...]
    h = _rms(x, g_ref[...]).astype(BF16)
    acc = x
    for f in range(wu_ref.shape[1] // ft):
        hid = jnp.dot(h, wu_ref[:, f * ft:(f + 1) * ft], preferred_element_type=F32)
        hid = jnp.square(jnp.maximum(hid, 0.0)).astype(BF16)
        acc = acc + jnp.dot(hid, wd_ref[f * ft:(f + 1) * ft, :], preferred_element_type=F32)
    if final_norm:
        acc = _rms(acc, gf_ref[...])
    o_ref[...] = acc


def _mlp(x2, g, w_up, w_down, g_final, final_norm):
    T, D = x2.shape
    F = w_up.shape[1]
    tm = min(512, T)
    kern = functools.partial(_mlp_kernel, ft=1024, final_norm=final_norm)
    return pl.pallas_call(
        kern,
        grid=(T // tm,),
        in_specs=[
            pl.BlockSpec((tm, D), lambda i: (i, 0)),
            pl.BlockSpec((1, D), lambda i: (0, 0)),
            pl.BlockSpec((D, F), lambda i: (0, 0)),
            pl.BlockSpec((F, D), lambda i: (0, 0)),
            pl.BlockSpec((1, D), lambda i: (0, 0)),
        ],
        out_specs=pl.BlockSpec((tm, D), lambda i: (i, 0)),
        out_shape=jax.ShapeDtypeStruct((T, D), F32),
        compiler_params=_params("parallel"),
        name="mlp",
    )(x2, g.reshape(1, D).astype(F32), w_up, w_down, g_final.reshape(1, D).astype(F32))


def kernel(x, w_in, g_mix, da_lambda, da_subln_g, rel_bias, sg_ln_g, sg_ln_b, sg_w, sg_b, lru_conv_w, lru_conv_b, lru_w_a, lru_b_a, lru_w_i, lru_b_i, lru_lambda, b_gate, w_branch, w_out, g_mlp, w_up, w_down, g_final):
    B, S, D = x.shape
    T = B * S
    depth = w_in.shape[0]
    x2 = x.reshape(T, D).astype(F32)

    col_u, col_v, col_r, col_g, col_gate = 3, 4, 5, 6, 7
    qscale = jnp.concatenate([jnp.full((D,), DA_HEAD_DIM ** -0.5, F32),
                              jnp.ones((w_in.shape[2] - D,), F32)])
    bias_tiles, cfar = _bias_tiles(rel_bias, ATT_TILE)

    for l in range(depth):
        w_l = (w_in[l] * qscale[None, :]).astype(BF16)
        proj = _inproj(x2, g_mix[l].astype(F32), w_l)
        ya = _attention(proj, B, S, da_lambda[l], da_subln_g[l], bias_tiles, cfar, l)
        yb = _sgu(proj, T, sg_ln_g[l], sg_ln_b[l], sg_w[l], sg_b[l], col_u, col_v)
        yc = _lru(proj, B, S, lru_conv_w[l], lru_conv_b[l], lru_w_a[l], lru_b_a[l],
                  lru_w_i[l], lru_b_i[l], lru_lambda[l], col_r, col_g)
        x2 = _merge(ya, yb, yc, proj, col_gate, b_gate[l], w_branch[l].astype(BF16),
                    w_out[l].astype(BF16), x2)
        x2 = _mlp(x2, g_mlp[l], w_up[l].astype(BF16), w_down[l].astype(BF16), g_final,
                  final_norm=(l == depth - 1))
    return x2.reshape(B, S, D).astype(x.dtype)
```

```python
import functools
import math

import jax
import jax.numpy as jnp
from jax import lax
from jax.experimental import pallas as pl
from jax.experimental.pallas import tpu as pltpu

EPS = 1e-6
CHUNK = 64
DA_HEADS = 8
DA_HEAD_DIM = 64
DA_V_DIM = 2 * DA_HEAD_DIM
REL_BUCKETS = 32
REL_MAX_DIST = 128
SG_BLOCK = 128
SG_GROUPS = 4
LRU_BLOCKS = 8
CONV_WIDTH = 4
LRU_C = 8.0
N_BRANCH = 3

LANES = 128
SUBLANES = 8
VMEM_LIMIT = 56 * 1024 * 1024
NEG = -1e30
LOG2E = math.log2(math.e)

ATT_TILE = 512
BF16 = jnp.bfloat16
F32 = jnp.float32


def _params(*sem):
    return pltpu.CompilerParams(dimension_semantics=sem, vmem_limit_bytes=VMEM_LIMIT)


def _rms(x, g):
    ms = jnp.mean(x * x, axis=-1, keepdims=True)
    return x * lax.rsqrt(ms + EPS) * g


def _inproj_kernel(x_ref, g_ref, w_ref, o_ref, h_ref):
    @pl.when(pl.program_id(1) == 0)
    def _():
        h_ref[...] = _rms(x_ref[...], g_ref[...]).astype(BF16)

    o_ref[...] = jnp.dot(h_ref[...], w_ref[...], preferred_element_type=F32).astype(o_ref.dtype)


def _inproj(x2, g, w):
    T, D = x2.shape
    N = w.shape[1]
    tm = min(2048, T)
    tn = 1024
    return pl.pallas_call(
        _inproj_kernel,
        grid=(T // tm, N // tn),
        in_specs=[
            pl.BlockSpec((tm, D), lambda i, j: (i, 0)),
            pl.BlockSpec((1, D), lambda i, j: (0, 0)),
            pl.BlockSpec((D, tn), lambda i, j: (0, j)),
        ],
        out_specs=pl.BlockSpec((tm, tn), lambda i, j: (i, j)),
        out_shape=jax.ShapeDtypeStruct((T, N), BF16),
        scratch_shapes=[pltpu.VMEM((tm, D), BF16)],
        compiler_params=_params("parallel", "arbitrary"),
        name="inproj",
    )(x2, g.reshape(1, D), w)


def _attn_kernel(cfar_ref, lp_ref, q_ref, k_ref, v_ref, bias_ref, g_ref, o_ref,
                 k1_ref, k2_ref, vt_ref, q1t_ref, q2t_ref, sa_ref, sb_ref, m_ref, l_ref, a_ref,
                 *, seq, tile, lam_init):
    h = pl.program_id(0)
    nq = seq // tile
    grp = tile // SUBLANES
    lane = lax.broadcasted_iota(jnp.int32, (1, LANES), 1)
    lo_half = lane < DA_HEAD_DIM

    c = jnp.full((1, LANES), cfar_ref[h], F32)
    c_hi = c.astype(BF16).astype(F32)
    c_lo = c - c_hi
    kfill1 = jnp.where(lane == DA_HEAD_DIM, c_hi, jnp.where(lane == DA_HEAD_DIM + 1, c_lo, 0.0))
    kfill2 = jnp.where(lane == 0, c_hi, jnp.where(lane == 1, c_lo, 0.0))
    qfill1 = jnp.where((lane == DA_HEAD_DIM) | (lane == DA_HEAD_DIM + 1), 1.0, 0.0)
    qfill2 = jnp.where((lane == 0) | (lane == 1), 1.0, 0.0)

    kk = k_ref[...].astype(F32)
    k1_ref[...] = jnp.where(lo_half, kk, kfill1).astype(BF16)
    k2_ref[...] = jnp.where(lo_half, kfill2, kk).astype(BF16)
    for c0 in range(0, seq, tile):
        vt_ref[:, c0:c0 + tile] = v_ref[c0:c0 + tile, :].astype(F32).T.astype(BF16)

    lp = lp_ref[...]
    lam = (jnp.exp(jnp.sum(lp[0:1] * lp[1:2], axis=-1, keepdims=True))
           - jnp.exp(jnp.sum(lp[2:3] * lp[3:4], axis=-1, keepdims=True)) + lam_init)

    def scores(j, dst_ref):
        k0 = pl.multiple_of(j * tile, tile)
        dst_ref[0] = jnp.dot(k1_ref[pl.ds(k0, tile), :], q1t_ref[...], preferred_element_type=F32)
        dst_ref[1] = jnp.dot(k2_ref[pl.ds(k0, tile), :], q2t_ref[...], preferred_element_type=F32)

    def softmax_pv(j, src_ref, bias):
        k0 = pl.multiple_of(j * tile, tile)
        vt = vt_ref[:, pl.ds(k0, tile)]
        for m in range(2):
            s = src_ref[m]
            if bias is not None:
                s = s + bias
            m_prev = m_ref[m]
            m_cur = jnp.max(s.reshape(grp, SUBLANES, tile), axis=0)
            m_new = jnp.maximum(m_prev, jnp.max(m_cur, axis=0, keepdims=True))
            alpha = jnp.exp2(m_prev - m_new)
            p = jnp.exp2(s - m_new)
            l_ref[m] = alpha * l_ref[m] + jnp.sum(p.reshape(grp, SUBLANES, tile), axis=0)
            a_ref[m] = alpha * a_ref[m] + jnp.dot(vt, p.astype(BF16), preferred_element_type=F32)
            m_ref[m] = m_new

    def q_body(i, carry):
        q0 = pl.multiple_of(i * tile, tile)
        qq = q_ref[pl.ds(q0, tile), :].astype(F32)
        q1t_ref[...] = jnp.where(lo_half, qq, qfill1).T.astype(BF16)
        q2t_ref[...] = jnp.where(lo_half, qfill2, qq).T.astype(BF16)
        m_ref[...] = jnp.full(m_ref.shape, NEG, F32)
        l_ref[...] = jnp.zeros(l_ref.shape, F32)
        a_ref[...] = jnp.zeros(a_ref.shape, F32)

        even = (i % 2) == 0

        @pl.when(even)
        def _():
            scores(0, sa_ref)

        @pl.when(jnp.logical_not(even))
        def _():
            scores(0, sb_ref)

        @pl.when(even & (i >= 2))
        def _():
            scores(1, sb_ref)
            softmax_pv(0, sa_ref, None)

        j_first = jnp.where(even, 1, 0)

        def far_pair(t, c2):
            j = j_first + 2 * t
            scores(j + 1, sa_ref)
            softmax_pv(j, sb_ref, None)
            scores(j + 2, sb_ref)
            softmax_pv(j + 1, sa_ref, None)
            return c2

        lax.fori_loop(0, jnp.maximum(i - 1, 0) // 2, far_pair, 0)

        @pl.when(i >= 1)
        def _():
            scores(i, sa_ref)
            softmax_pv(i - 1, sb_ref, bias_ref[1])

        softmax_pv(i, sa_ref, bias_ref[0])

        r1 = 1.0 / jnp.sum(l_ref[0], axis=0, keepdims=True)
        r2 = 1.0 / jnp.sum(l_ref[1], axis=0, keepdims=True)
        ot = a_ref[0] * r1 - a_ref[1] * (lam * r2)
        ms = jnp.mean(ot * ot, axis=0, keepdims=True)
        y = (ot * lax.rsqrt(ms + EPS)).T * g_ref[...] * (1.0 - lam_init)
        o_ref[pl.ds(q0, tile), :] = y.astype(o_ref.dtype)
        return carry

    lax.fori_loop(0, nq, q_body, 0)


def _t5_bucket(rel):
    nb = REL_BUCKETS // 2
    max_exact = nb // 2
    ret = (rel > 0).astype(jnp.int32) * nb
    n = jnp.abs(rel)
    nf = jnp.maximum(n, 1).astype(jnp.float32)
    large = max_exact + (jnp.log(nf / max_exact) / math.log(REL_MAX_DIST / max_exact)
                         * (nb - max_exact)).astype(jnp.int32)
    large = jnp.minimum(large, nb - 1)
    return ret + jnp.where(n < max_exact, n, large)


def _bias_tiles(rel_table, tile):
    pos = jnp.arange(tile, dtype=jnp.int32)
    table = rel_table.astype(F32) * LOG2E
    cfar = table[_t5_bucket(jnp.int32(-(tile + 1)))]
    tiles = []
    for d in (0, 1):
        rel = (pos[None, :] - d * tile) - pos[:, None]
        b = table[_t5_bucket(rel)].transpose(2, 0, 1) - cfar[:, None, None]
        if d == 0:
            allowed = (pos[None, :] // CHUNK) <= (pos[:, None] // CHUNK)
            b = jnp.where(allowed[None], b, NEG)
        tiles.append(b.transpose(0, 2, 1))
    return jnp.stack(tiles, axis=1), cfar


def _attention(proj, B, S, lam_params, subln_g, bias_tiles, cfar, layer_idx):
    T = B * S
    tile = ATT_TILE
    assert tile + 1 >= REL_MAX_DIST and S % tile == 0
    lam_init = 0.8 - 0.6 * math.exp(-0.3 * layer_idx)
    kern = functools.partial(_attn_kernel, seq=S, tile=tile, lam_init=lam_init)
    H = DA_HEADS
    return pl.pallas_call(
        kern,
        grid=(H, B),
        in_specs=[
            pl.BlockSpec(memory_space=pltpu.SMEM),
            pl.BlockSpec((4, DA_HEAD_DIM), lambda h, b: (0, 0)),
            pl.BlockSpec((S, LANES), lambda h, b: (b, h)),
            pl.BlockSpec((S, LANES), lambda h, b: (b, H + h)),
            pl.BlockSpec((S, LANES), lambda h, b: (b, 2 * H + h)),
            pl.BlockSpec((None, 2, tile, tile), lambda h, b: (h, 0, 0, 0)),
            pl.BlockSpec((1, DA_V_DIM), lambda h, b: (0, 0)),
        ],
        out_specs=pl.BlockSpec((S, LANES), lambda h, b: (b, h)),
        out_shape=jax.ShapeDtypeStruct((T, H * DA_V_DIM), BF16),
        scratch_shapes=[
            pltpu.VMEM((S, LANES), BF16), pltpu.VMEM((S, LANES), BF16), pltpu.VMEM((DA_V_DIM, S), BF16),
            pltpu.VMEM((LANES, tile), BF16), pltpu.VMEM((LANES, tile), BF16),
            pltpu.VMEM((2, tile, tile), F32), pltpu.VMEM((2, tile, tile), F32),
            pltpu.VMEM((2, 1, tile), F32), pltpu.VMEM((2, SUBLANES, tile), F32),
            pltpu.VMEM((2, DA_V_DIM, tile), F32),
        ],
        compiler_params=_params("parallel", "parallel"),
        name="diff_attn",
    )(cfar, lam_params.astype(F32), proj, proj, proj, bias_tiles, subln_g.reshape(1, DA_V_DIM).astype(F32))


def _sgu_kernel(u_ref, v_ref, g_ref, b_ref, w_ref, bs_ref, o_ref, *, rows):
    v = jax.nn.gelu(v_ref[...].astype(F32))
    mu = jnp.mean(v, axis=-1, keepdims=True)
    vc = v - mu
    var = jnp.mean(vc * vc, axis=-1, keepdims=True)
    vn = (vc * lax.rsqrt(var + EPS) * g_ref[...] + b_ref[...]).astype(BF16)
    pos_q = lax.broadcasted_iota(jnp.int32, (SG_BLOCK, SG_BLOCK), 0)
    pos_k = lax.broadcasted_iota(jnp.int32, (SG_BLOCK, SG_BLOCK), 1)
    mask = (pos_k // CHUNK) <= (pos_q // CHUNK)
    gd = v.shape[1] // SG_GROUPS
    for g in range(SG_GROUPS):
        w = jnp.where(mask, w_ref[g], 0.0).astype(BF16)
        cs = slice(g * gd, (g + 1) * gd)
        for n in range(rows // SG_BLOCK):
            rs = slice(n * SG_BLOCK, (n + 1) * SG_BLOCK)
            mixed = jnp.dot(w, vn[rs, cs], preferred_element_type=F32) + bs_ref[:, cs]
            u = jax.nn.gelu(u_ref[rs, cs].astype(F32))
            o_ref[rs, cs] = (u * mixed).astype(o_ref.dtype)


def _sgu(proj, T, ln_g, ln_b, w_s, b_s, col_u, col_v):
    W = ln_g.shape[0]
    tm = min(512, T)
    gd = W // SG_GROUPS
    bsb = jnp.repeat(b_s.T.astype(F32), gd, axis=1)
    kern = functools.partial(_sgu_kernel, rows=tm)
    return pl.pallas_call(
        kern,
        grid=(T // tm,),
        in_specs=[
            pl.BlockSpec((tm, W), lambda i: (i, col_u)),
            pl.BlockSpec((tm, W), lambda i: (i, col_v)),
            pl.BlockSpec((1, W), lambda i: (0, 0)),
            pl.BlockSpec((1, W), lambda i: (0, 0)),
            pl.BlockSpec((SG_GROUPS, SG_BLOCK, SG_BLOCK), lambda i: (0, 0, 0)),
            pl.BlockSpec((SG_BLOCK, W), lambda i: (0, 0)),
        ],
        out_specs=pl.BlockSpec((tm, W), lambda i: (i, 0)),
        out_shape=jax.ShapeDtypeStruct((T, W), BF16),
        compiler_params=_params("parallel"),
        name="sgu",
    )(proj, proj, ln_g.reshape(1, W).astype(F32), ln_b.reshape(1, W).astype(F32),
      w_s.astype(F32), bsb)


def _lru_kernel(xr_ref, prev_ref, xg_ref, cw_ref, cb_ref, wa_ref, ba_ref, wi_ref, bi_ref, lam_ref,
                o_ref, ext_ref, a_ref, b_ref, h_ref, *, rows):
    t = pl.program_id(1)
    C = xr_ref.shape[1]
    halo = SUBLANES

    @pl.when(t == 0)
    def _():
        h_ref[...] = jnp.zeros(h_ref.shape, F32)

    prev = prev_ref[...].astype(F32)
    ext_ref[0:halo, :] = jnp.where(t == 0, jnp.zeros_like(prev), prev)
    ext_ref[halo:halo + rows, :] = xr_ref[...].astype(F32)

    xc = jnp.zeros((rows, C), F32) + cb_ref[...]
    for j in range(CONV_WIDTH):
        off = halo - (CONV_WIDTH - 1) + j
        xc = xc + cw_ref[j:j + 1, :] * ext_ref[off:off + rows, :]

    z = lam_ref[...]
    nz = -z
    softplus_neg = jnp.maximum(nz, 0.0) + jnp.log1p(jnp.exp(-jnp.abs(nz)))
    xcb = xc.astype(BF16)
    bd = C // LRU_BLOCKS
    r_parts, i_parts = [], []
    for hb in range(LRU_BLOCKS):
        cs = slice(hb * bd, (hb + 1) * bd)
        r_parts.append(jnp.dot(xcb[:, cs], wa_ref[hb], preferred_element_type=F32))
        i_parts.append(jnp.dot(xcb[:, cs], wi_ref[hb], preferred_element_type=F32))
    r = jax.nn.sigmoid(jnp.concatenate(r_parts, axis=1) + ba_ref[...])
    ig = jax.nn.sigmoid(jnp.concatenate(i_parts, axis=1) + bi_ref[...])
    log_a = -LRU_C * r * softplus_neg
    a = jnp.exp(log_a)
    mult = jnp.sqrt(jnp.maximum(-jnp.tanh(log_a) * (a * a + 1.0), 0.0))
    u = xc * ig * mult

    a3 = a.reshape(rows // SUBLANES, SUBLANES, C)
    b3 = u.reshape(rows // SUBLANES, SUBLANES, C)
    row = lax.broadcasted_iota(jnp.int32, (1, SUBLANES, C), 1)
    s = 1
    while s < SUBLANES:
        keep = row >= s
        a_sh = pltpu.roll(a3, s, axis=1)
        b_sh = pltpu.roll(b3, s, axis=1)
        b3 = jnp.where(keep, b3 + a3 * b_sh, b3)
        a3 = jnp.where(keep, a3 * a_sh, a3)
        s *= 2
    a_ref[...] = a3.reshape(rows, C)
    b_ref[...] = b3.reshape(rows, C)

    def grp(i, hprev):
        r0 = pl.multiple_of(i * SUBLANES, SUBLANES)
        hh = b_ref[pl.ds(r0, SUBLANES), :] + a_ref[pl.ds(r0, SUBLANES), :] * hprev
        gate = jax.nn.gelu(xg_ref[pl.ds(r0, SUBLANES), :].astype(F32))
        o_ref[pl.ds(r0, SUBLANES), :] = (hh * gate).astype(o_ref.dtype)
        return jnp.broadcast_to(hh[SUBLANES - 1:SUBLANES, :], (SUBLANES, C))

    hlast = lax.fori_loop(0, rows // SUBLANES, grp, jnp.broadcast_to(h_ref[...], (SUBLANES, C)))
    h_ref[...] = hlast[0:1, :]


def _lru(proj, B, S, conv_w, conv_b, w_a, b_a, w_i, b_i, lam, col_r, col_g):
    T = B * S
    C = lam.shape[0]
    ts = min(512, S)
    nt = S // ts
    hb = ts // SUBLANES
    kern = functools.partial(_lru_kernel, rows=ts)
    vec = lambda a: a.reshape(1, C).astype(F32)
    const2 = lambda b, t: (0, 0)
    return pl.pallas_call(
        kern,
        grid=(B, nt),
        in_specs=[
            pl.BlockSpec((ts, C), lambda b, t: (b * nt + t, col_r)),
            pl.BlockSpec((SUBLANES, C),
                         lambda b, t: (jnp.maximum((b * nt + t) * hb - 1, 0), col_r)),
            pl.BlockSpec((ts, C), lambda b, t: (b * nt + t, col_g)),
            pl.BlockSpec((CONV_WIDTH, C), const2),
            pl.BlockSpec((1, C), const2),
            pl.BlockSpec((LRU_BLOCKS, C // LRU_BLOCKS, C // LRU_BLOCKS), lambda b, t: (0, 0, 0)),
            pl.BlockSpec((1, C), const2),
            pl.BlockSpec((LRU_BLOCKS, C // LRU_BLOCKS, C // LRU_BLOCKS), lambda b, t: (0, 0, 0)),
            pl.BlockSpec((1, C), const2),
            pl.BlockSpec((1, C), const2),
        ],
        out_specs=pl.BlockSpec((ts, C), lambda b, t: (b * nt + t, 0)),
        out_shape=jax.ShapeDtypeStruct((T, C), BF16),
        scratch_shapes=[
            pltpu.VMEM((ts + SUBLANES, C), F32),
            pltpu.VMEM((ts, C), F32),
            pltpu.VMEM((ts, C), F32),
            pltpu.VMEM((1, C), F32),
        ],
        compiler_params=_params("parallel", "arbitrary"),
        name="rglru",
    )(proj, proj, proj, conv_w.astype(F32), vec(conv_b), w_a.astype(BF16), vec(b_a),
      w_i.astype(BF16), vec(b_i), vec(lam))


def _merge_kernel(ya_ref, yb_ref, yc_ref, g0_ref, g1_ref, g2_ref, bg_ref, wb_ref, wo_ref, x_ref, o_ref):
    merged = None
    for k, (y_ref, g_ref) in enumerate(((ya_ref, g0_ref), (yb_ref, g1_ref), (yc_ref, g2_ref))):
        gate = jax.nn.sigmoid(g_ref[...].astype(F32) + bg_ref[k:k + 1, :])
        term = gate * jnp.dot(y_ref[...], wb_ref[k], preferred_element_type=F32)
        merged = term if merged is None else merged + term
    o_ref[...] = x_ref[...] + jnp.dot(merged.astype(BF16), wo_ref[...], preferred_element_type=F32)


def _merge(ya, yb, yc, proj, col_gate, b_gate, w_branch, w_out, x2):
    T, D = x2.shape
    tm = min(512, T)
    tok = lambda i: (i, 0)
    return pl.pallas_call(
        _merge_kernel,
        grid=(T // tm,),
        in_specs=[
            pl.BlockSpec((tm, D), tok), pl.BlockSpec((tm, D), tok), pl.BlockSpec((tm, D), tok),
            pl.BlockSpec((tm, D), lambda i: (i, col_gate)),
            pl.BlockSpec((tm, D), lambda i: (i, col_gate + 1)),
            pl.BlockSpec((tm, D), lambda i: (i, col_gate + 2)),
            pl.BlockSpec((N_BRANCH, D), lambda i: (0, 0)),
            pl.BlockSpec((N_BRANCH, D, D), lambda i: (0, 0, 0)),
            pl.BlockSpec((D, D), lambda i: (0, 0)),
            pl.BlockSpec((tm, D), tok),
        ],
        out_specs=pl.BlockSpec((tm, D), tok),
        out_shape=jax.ShapeDtypeStruct((T, D), F32),
        compiler_params=_params("parallel"),
        name="merge",
    )(ya, yb, yc, proj, proj, proj, b_gate.astype(F32), w_branch, w_out, x2)


def _mlp_kernel(x_ref, g_ref, wu_ref, wd_ref, gf_ref, o_ref, *, ft, final_norm):
    x = x_ref[...]
    h = _rms(x, g_ref[...]).astype(BF16)
    acc = x
    for f in range(wu_ref.shape[1] // ft):
        hid = jnp.dot(h, wu_ref[:, f * ft:(f + 1) * ft], preferred_element_type=F32)
        hid = jnp.square(jnp.maximum(hid, 0.0)).astype(BF16)
        acc = acc + jnp.dot(hid, wd_ref[f * ft:(f + 1) * ft, :], preferred_element_type=F32)
    if final_norm:
        acc = _rms(acc, gf_ref[...])
    o_ref[...] = acc


def _mlp(x2, g, w_up, w_down, g_final, final_norm):
    T, D = x2.shape
    F = w_up.shape[1]
    tm = min(512, T)
    kern = functools.partial(_mlp_kernel, ft=1024, final_norm=final_norm)
    return pl.pallas_call(
        kern,
        grid=(T // tm,),
        in_specs=[
            pl.BlockSpec((tm, D), lambda i: (i, 0)),
            pl.BlockSpec((1, D), lambda i: (0, 0)),
            pl.BlockSpec((D, F), lambda i: (0, 0)),
            pl.BlockSpec((F, D), lambda i: (0, 0)),
            pl.BlockSpec((1, D), lambda i: (0, 0)),
        ],
        out_specs=pl.BlockSpec((tm, D), lambda i: (i, 0)),
        out_shape=jax.ShapeDtypeStruct((T, D), F32),
        compiler_params=_params("parallel"),
        name="mlp",
    )(x2, g.reshape(1, D).astype(F32), w_up, w_down, g_final.reshape(1, D).astype(F32))


def kernel(x, w_in, g_mix, da_lambda, da_subln_g, rel_bias, sg_ln_g, sg_ln_b, sg_w, sg_b, lru_conv_w, lru_conv_b, lru_w_a, lru_b_a, lru_w_i, lru_b_i, lru_lambda, b_gate, w_branch, w_out, g_mlp, w_up, w_down, g_final):
    B, S, D = x.shape
    T = B * S
    depth = w_in.shape[0]
    x2 = x.reshape(T, D).astype(F32)

    col_u, col_v, col_r, col_g, col_gate = 3, 4, 5, 6, 7
    qscale = jnp.concatenate([jnp.full((D,), DA_HEAD_DIM ** -0.5 * LOG2E, F32),
                              jnp.ones((w_in.shape[2] - D,), F32)])
    bias_tiles, cfar = _bias_tiles(rel_bias, ATT_TILE)

    for l in range(depth):
        w_l = (w_in[l] * qscale[None, :]).astype(BF16)
        proj = _inproj(x2, g_mix[l].astype(F32), w_l)
        ya = _attention(proj, B, S, da_lambda[l], da_subln_g[l], bias_tiles, cfar, l)
        yb = _sgu(proj, T, sg_ln_g[l], sg_ln_b[l], sg_w[l], sg_b[l], col_u, col_v)
        yc = _lru(proj, B, S, lru_conv_w[l], lru_conv_b[l], lru_w_a[l], lru_b_a[l],
                  lru_w_i[l], lru_b_i[l], lru_lambda[l], col_r, col_g)
        x2 = _merge(ya, yb, yc, proj, col_gate, b_gate[l], w_branch[l].astype(BF16),
                    w_out[l].astype(BF16), x2)
        x2 = _mlp(x2, g_mlp[l], w_up[l].astype(BF16), w_down[l].astype(BF16), g_final,
                  final_norm=(l == depth - 1))
    return x2.reshape(B, S, D).astype(x.dtype)
```

```python
import functools
import math

import jax
import jax.numpy as jnp
from jax import lax
from jax.experimental import pallas as pl
from jax.experimental.pallas import tpu as pltpu

EPS = 1e-6
CHUNK = 64
DA_HEADS = 8
DA_HEAD_DIM = 64
DA_V_DIM = 2 * DA_HEAD_DIM
REL_BUCKETS = 32
REL_MAX_DIST = 128
SG_BLOCK = 128
SG_GROUPS = 4
LRU_BLOCKS = 8
CONV_WIDTH = 4
LRU_C = 8.0
N_BRANCH = 3

LANES = 128
SUBLANES = 8
VMEM_LIMIT = 56 * 1024 * 1024
NEG = -1e30
LOG2E = math.log2(math.e)

ATT_TILE = 512
BF16 = jnp.bfloat16
F32 = jnp.float32


def _params(*sem):
    return pltpu.CompilerParams(dimension_semantics=sem, vmem_limit_bytes=VMEM_LIMIT)


def _rms(x, g):
    ms = jnp.mean(x * x, axis=-1, keepdims=True)
    return x * lax.rsqrt(ms + EPS) * g


def _inproj_kernel(x_ref, g_ref, w_ref, o_ref, h_ref):
    @pl.when(pl.program_id(1) == 0)
    def _():
        h_ref[...] = _rms(x_ref[...], g_ref[...]).astype(BF16)

    o_ref[...] = jnp.dot(h_ref[...], w_ref[...], preferred_element_type=F32).astype(o_ref.dtype)


def _inproj(x2, g, w):
    T, D = x2.shape
    N = w.shape[1]
    tm = min(2048, T)
    tn = 1024
    return pl.pallas_call(
        _inproj_kernel,
        grid=(T // tm, N // tn),
        in_specs=[
            pl.BlockSpec((tm, D), lambda i, j: (i, 0)),
            pl.BlockSpec((1, D), lambda i, j: (0, 0)),
            pl.BlockSpec((D, tn), lambda i, j: (0, j)),
        ],
        out_specs=pl.BlockSpec((tm, tn), lambda i, j: (i, j)),
        out_shape=jax.ShapeDtypeStruct((T, N), BF16),
        scratch_shapes=[pltpu.VMEM((tm, D), BF16)],
        compiler_params=_params("parallel", "arbitrary"),
        name="inproj",
    )(x2, g.reshape(1, D), w)


def _attn_kernel(cfar_ref, lp_ref, q_ref, k_ref, v_ref, dblk_ref, corner_ref, g_ref, o_ref,
                 k1_ref, k2_ref, vt_ref, qt_ref, sa_ref, sb_ref, sc_ref, mca_ref, mcb_ref, mcc_ref,
                 m_ref, l_ref, a_ref,
                 *, seq, tile, lam_init):
    h = pl.program_id(0)
    nq = seq // tile
    grp = tile // SUBLANES
    lane = lax.broadcasted_iota(jnp.int32, (1, LANES), 1)
    lo_half = lane < DA_HEAD_DIM

    c = jnp.full((1, LANES), cfar_ref[h], F32)
    c_hi = c.astype(BF16).astype(F32)
    c_lo = c - c_hi
    kfill1 = jnp.where(lane == DA_HEAD_DIM, c_hi, jnp.where(lane == DA_HEAD_DIM + 1, c_lo, 0.0))
    kfill2 = jnp.where(lane == 0, c_hi, jnp.where(lane == 1, c_lo, 0.0))
    qfill1 = jnp.where((lane == DA_HEAD_DIM) | (lane == DA_HEAD_DIM + 1), 1.0, 0.0)
    qfill2 = jnp.where((lane == 0) | (lane == 1), 1.0, 0.0)

    kk = k_ref[...].astype(F32)
    k1_ref[...] = jnp.where(lo_half, kk, kfill1).astype(BF16)
    k2_ref[...] = jnp.where(lo_half, kfill2, kk).astype(BF16)
    for c0 in range(0, seq, tile):
        vt_ref[:, c0:c0 + tile] = v_ref[c0:c0 + tile, :].astype(F32).T.astype(BF16)

    lp = lp_ref[...]
    lam = (jnp.exp(jnp.sum(lp[0:1] * lp[1:2], axis=-1, keepdims=True))
           - jnp.exp(jnp.sum(lp[2:3] * lp[3:4], axis=-1, keepdims=True)) + lam_init)

    def load_queries(i):
        q0 = pl.multiple_of(i * tile, tile)
        qq = q_ref[pl.ds(q0, tile), :].astype(F32)
        qt_ref[i % 2, 0] = jnp.where(lo_half, qq, qfill1).T.astype(BF16)
        qt_ref[i % 2, 1] = jnp.where(lo_half, qfill2, qq).T.astype(BF16)

    def scores(i, j, dst):
        s_ref, mc_ref = dst
        k0 = pl.multiple_of(j * tile, tile)
        for m, ka_ref in enumerate((k1_ref, k2_ref)):
            s = jnp.dot(ka_ref[pl.ds(k0, tile), :], qt_ref[i % 2, m], preferred_element_type=F32)
            s_ref[m] = s
            mc_ref[m] = jnp.max(s.reshape(grp, SUBLANES, tile), axis=0)

    half = tile // 2

    def scores_diag(i, dst):
        s_ref, _ = dst
        k0 = pl.multiple_of(i * tile, tile)
        k1 = pl.multiple_of(i * tile + half, half)
        for m, ka_ref in enumerate((k1_ref, k2_ref)):
            s_ref[m, 0:half, :] = jnp.dot(ka_ref[pl.ds(k0, half), :], qt_ref[i % 2, m],
                                          preferred_element_type=F32)
            s_ref[m, half:tile, half:tile] = jnp.dot(ka_ref[pl.ds(k1, half), :],
                                                     qt_ref[i % 2, m, :, half:tile],
                                                     preferred_element_type=F32)

    def softmax_diag(i, src):
        s_ref, _ = src
        k0 = pl.multiple_of(i * tile, tile)
        k1 = pl.multiple_of(i * tile + half, half)
        hgrp = half // SUBLANES
        colmax = lambda x: jnp.max(x.reshape(hgrp, SUBLANES, half), axis=0)
        colsum = lambda x: jnp.sum(x.reshape(hgrp, SUBLANES, half), axis=0)
        dblk = dblk_ref[...]
        vt_lo = vt_ref[:, pl.ds(k0, half)]
        vt_hi = vt_ref[:, pl.ds(k1, half)]
        for m in range(2):
            s_lo = s_ref[m, 0:half, 0:half] + dblk
            m_prev = m_ref[m, :, 0:half]
            m_new = jnp.maximum(m_prev, jnp.max(colmax(s_lo), axis=0, keepdims=True))
            alpha = jnp.exp2(m_prev - m_new)
            p_lo = jnp.exp2(s_lo - m_new)
            l_ref[m, :, 0:half] = alpha * l_ref[m, :, 0:half] + colsum(p_lo)
            a_ref[m, :, 0:half] = alpha * a_ref[m, :, 0:half] + jnp.dot(
                vt_lo, p_lo.astype(BF16), preferred_element_type=F32)
            m_ref[m, :, 0:half] = m_new
            s_ref[m, half - LANES:half, half:half + LANES] = (
                s_ref[m, half - LANES:half, half:half + LANES] + corner_ref[...])
            s_top = s_ref[m, 0:half, half:tile]
            s_bot = s_ref[m, half:tile, half:tile] + dblk
            m_prev = m_ref[m, :, half:tile]
            m_cur = jnp.maximum(colmax(s_top), colmax(s_bot))
            m_new = jnp.maximum(m_prev, jnp.max(m_cur, axis=0, keepdims=True))
            alpha = jnp.exp2(m_prev - m_new)
            p_top = jnp.exp2(s_top - m_new)
            p_bot = jnp.exp2(s_bot - m_new)
            l_ref[m, :, half:tile] = alpha * l_ref[m, :, half:tile] + colsum(p_top) + colsum(p_bot)
            a_ref[m, :, half:tile] = (alpha * a_ref[m, :, half:tile]
                                      + jnp.dot(vt_lo, p_top.astype(BF16), preferred_element_type=F32)
                                      + jnp.dot(vt_hi, p_bot.astype(BF16), preferred_element_type=F32))
            m_ref[m, :, half:tile] = m_new

    def near_fix(dst):
        s_ref, mc_ref = dst
        for m in range(2):
            s_ref[m, tile - LANES:tile, 0:LANES] = (s_ref[m, tile - LANES:tile, 0:LANES]
                                                    + corner_ref[...])
            col = s_ref[m, :, 0:LANES]
            mc_ref[m, :, 0:LANES] = jnp.max(col.reshape(grp, SUBLANES, LANES), axis=0)

    def softmax_pv(j, src):
        s_ref, mc_ref = src
        k0 = pl.multiple_of(j * tile, tile)
        vt = vt_ref[:, pl.ds(k0, tile)]
        for m in range(2):
            s = s_ref[m]
            m_prev = m_ref[m]
            m_new = jnp.maximum(m_prev, jnp.max(mc_ref[m], axis=0, keepdims=True))
            alpha = jnp.exp2(m_prev - m_new)
            p = jnp.exp2(s - m_new)
            l_ref[m] = alpha * l_ref[m] + jnp.sum(p.reshape(grp, SUBLANES, tile), axis=0)
            a_ref[m] = alpha * a_ref[m] + jnp.dot(vt, p.astype(BF16), preferred_element_type=F32)
            m_ref[m] = m_new

    sa = (sa_ref, mca_ref)
    sb = (sb_ref, mcb_ref)
    sc = (sc_ref, mcc_ref)

    load_queries(0)
    scores_diag(0, sa)

    def q_body(i, carry):
        m_ref[...] = jnp.full(m_ref.shape, NEG, F32)
        l_ref[...] = jnp.zeros(l_ref.shape, F32)
        a_ref[...] = jnp.zeros(a_ref.shape, F32)
        even = (i % 2) == 0
        nxt = jnp.minimum(i + 1, nq - 1)

        @pl.when(even & (i >= 2))
        def _():
            scores(i, 1, sb)
            softmax_pv(0, sc)

        j_first = jnp.where(even, 1, 0)

        def far_pair(t, c2):
            j = j_first + 2 * t
            scores(i, j + 1, sa)
            softmax_pv(j, sb)
            scores(i, j + 2, sb)
            softmax_pv(j + 1, sa)
            return c2

        lax.fori_loop(0, jnp.maximum(i - 1, 0) // 2, far_pair, 0)

        @pl.when(i >= 1)
        def _():
            scores_diag(i, sa)
            near_fix(sb)
            softmax_pv(i - 1, sb)

        @pl.when(even)
        def _():
            load_queries(nxt)
            scores(nxt, 0, sb)
            softmax_diag(i, sa)

        @pl.when(jnp.logical_not(even))
        def _():
            load_queries(nxt)
            scores(nxt, 0, sc)
            softmax_diag(i, sa)

        r1 = 1.0 / jnp.sum(l_ref[0], axis=0, keepdims=True)
        r2 = 1.0 / jnp.sum(l_ref[1], axis=0, keepdims=True)
        ot = a_ref[0] * r1 - a_ref[1] * (lam * r2)
        ms = jnp.mean(ot * ot, axis=0, keepdims=True)
        y = (ot * lax.rsqrt(ms + EPS)).T * g_ref[...] * (1.0 - lam_init)
        q0 = pl.multiple_of(i * tile, tile)
        o_ref[pl.ds(q0, tile), :] = y.astype(o_ref.dtype)
        return carry

    lax.fori_loop(0, nq, q_body, 0)


def _t5_bucket(rel):
    nb = REL_BUCKETS // 2
    max_exact = nb // 2
    ret = (rel > 0).astype(jnp.int32) * nb
    n = jnp.abs(rel)
    nf = jnp.maximum(n, 1).astype(jnp.float32)
    large = max_exact + (jnp.log(nf / max_exact) / math.log(REL_MAX_DIST / max_exact)
                         * (nb - max_exact)).astype(jnp.int32)
    large = jnp.minimum(large, nb - 1)
    return ret + jnp.where(n < max_exact, n, large)


def _bias_tiles(rel_table, block):
    table = rel_table.astype(F32) * LOG2E
    cfar = table[_t5_bucket(jnp.int32(-REL_MAX_DIST))]
    pos = jnp.arange(block, dtype=jnp.int32)
    rel = pos[None, :] - pos[:, None]
    allowed = (pos[None, :] // CHUNK) <= (pos[:, None] // CHUNK)
    diag = table[_t5_bucket(rel)].transpose(2, 0, 1) - cfar[:, None, None]
    diag = jnp.where(allowed[None], diag, NEG).transpose(0, 2, 1)
    cpos = jnp.arange(LANES, dtype=jnp.int32)
    crel = (cpos[None, :] - LANES) - cpos[:, None]
    corner = (table[_t5_bucket(crel)].transpose(2, 0, 1) - cfar[:, None, None]).transpose(0, 2, 1)
    return diag, corner, cfar


def _attention(proj, B, S, lam_params, subln_g, bias_diag, bias_corner, cfar, layer_idx):
    T = B * S
    tile = ATT_TILE
    assert tile // 2 >= LANES >= REL_MAX_DIST and S % tile == 0
    lam_init = 0.8 - 0.6 * math.exp(-0.3 * layer_idx)
    kern = functools.partial(_attn_kernel, seq=S, tile=tile, lam_init=lam_init)
    H = DA_HEADS
    return pl.pallas_call(
        kern,
        grid=(H, B),
        in_specs=[
            pl.BlockSpec(memory_space=pltpu.SMEM),
            pl.BlockSpec((4, DA_HEAD_DIM), lambda h, b: (0, 0)),
            pl.BlockSpec((S, LANES), lambda h, b: (b, h)),
            pl.BlockSpec((S, LANES), lambda h, b: (b, H + h)),
            pl.BlockSpec((S, LANES), lambda h, b: (b, 2 * H + h)),
            pl.BlockSpec((None, tile // 2, tile // 2), lambda h, b: (h, 0, 0)),
            pl.BlockSpec((None, LANES, LANES), lambda h, b: (h, 0, 0)),
            pl.BlockSpec((1, DA_V_DIM), lambda h, b: (0, 0)),
        ],
        out_specs=pl.BlockSpec((S, LANES), lambda h, b: (b, h)),
        out_shape=jax.ShapeDtypeStruct((T, H * DA_V_DIM), BF16),
        scratch_shapes=[
            pltpu.VMEM((S, LANES), BF16), pltpu.VMEM((S, LANES), BF16), pltpu.VMEM((DA_V_DIM, S), BF16),
            pltpu.VMEM((2, 2, LANES, tile), BF16),
            pltpu.VMEM((2, tile, tile), F32), pltpu.VMEM((2, tile, tile), F32),
            pltpu.VMEM((2, tile, tile), F32),
            pltpu.VMEM((2, SUBLANES, tile), F32), pltpu.VMEM((2, SUBLANES, tile), F32),
            pltpu.VMEM((2, SUBLANES, tile), F32),
            pltpu.VMEM((2, 1, tile), F32), pltpu.VMEM((2, SUBLANES, tile), F32),
            pltpu.VMEM((2, DA_V_DIM, tile), F32),
        ],
        compiler_params=_params("parallel", "parallel"),
        name="diff_attn",
    )(cfar, lam_params.astype(F32), proj, proj, proj, bias_diag, bias_corner,
      subln_g.reshape(1, DA_V_DIM).astype(F32))


def _sgu_kernel(u_ref, v_ref, g_ref, b_ref, w_ref, bs_ref, o_ref, *, rows):
    v = jax.nn.gelu(v_ref[...].astype(F32))
    mu = jnp.mean(v, axis=-1, keepdims=True)
    vc = v - mu
    var = jnp.mean(vc * vc, axis=-1, keepdims=True)
    vn = (vc * lax.rsqrt(var + EPS) * g_ref[...] + b_ref[...]).astype(BF16)
    pos_q = lax.broadcasted_iota(jnp.int32, (SG_BLOCK, SG_BLOCK), 0)
    pos_k = lax.broadcasted_iota(jnp.int32, (SG_BLOCK, SG_BLOCK), 1)
    mask = (pos_k // CHUNK) <= (pos_q // CHUNK)
    gd = v.shape[1] // SG_GROUPS
    for g in range(SG_GROUPS):
        w = jnp.where(mask, w_ref[g], 0.0).astype(BF16)
        cs = slice(g * gd, (g + 1) * gd)
        for n in range(rows // SG_BLOCK):
            rs = slice(n * SG_BLOCK, (n + 1) * SG_BLOCK)
            mixed = jnp.dot(w, vn[rs, cs], preferred_element_type=F32) + bs_ref[:, cs]
            u = jax.nn.gelu(u_ref[rs, cs].astype(F32))
            o_ref[rs, cs] = (u * mixed).astype(o_ref.dtype)


def _sgu(proj, T, ln_g, ln_b, w_s, b_s, col_u, col_v):
    W = ln_g.shape[0]
    tm = min(512, T)
    gd = W // SG_GROUPS
    bsb = jnp.repeat(b_s.T.astype(F32), gd, axis=1)
    kern = functools.partial(_sgu_kernel, rows=tm)
    return pl.pallas_call(
        kern,
        grid=(T // tm,),
        in_specs=[
            pl.BlockSpec((tm, W), lambda i: (i, col_u)),
            pl.BlockSpec((tm, W), lambda i: (i, col_v)),
            pl.BlockSpec((1, W), lambda i: (0, 0)),
            pl.BlockSpec((1, W), lambda i: (0, 0)),
            pl.BlockSpec((SG_GROUPS, SG_BLOCK, SG_BLOCK), lambda i: (0, 0, 0)),
            pl.BlockSpec((SG_BLOCK, W), lambda i: (0, 0)),
        ],
        out_specs=pl.BlockSpec((tm, W), lambda i: (i, 0)),
        out_shape=jax.ShapeDtypeStruct((T, W), BF16),
        compiler_params=_params("parallel"),
        name="sgu",
    )(proj, proj, ln_g.reshape(1, W).astype(F32), ln_b.reshape(1, W).astype(F32),
      w_s.astype(F32), bsb)


def _lru_kernel(xr_ref, prev_ref, xg_ref, cw_ref, cb_ref, wa_ref, ba_ref, wi_ref, bi_ref, lam_ref,
                o_ref, ext_ref, a_ref, b_ref, h_ref, *, rows):
    t = pl.program_id(1)
    C = xr_ref.shape[1]
    halo = SUBLANES

    @pl.when(t == 0)
    def _():
        h_ref[...] = jnp.zeros(h_ref.shape, F32)

    prev = prev_ref[...].astype(F32)
    ext_ref[0:halo, :] = jnp.where(t == 0, jnp.zeros_like(prev), prev)
    ext_ref[halo:halo + rows, :] = xr_ref[...].astype(F32)

    xc = jnp.zeros((rows, C), F32) + cb_ref[...]
    for j in range(CONV_WIDTH):
        off = halo - (CONV_WIDTH - 1) + j
        xc = xc + cw_ref[j:j + 1, :] * ext_ref[off:off + rows, :]

    z = lam_ref[...]
    nz = -z
    softplus_neg = jnp.maximum(nz, 0.0) + jnp.log1p(jnp.exp(-jnp.abs(nz)))
    xcb = xc.astype(BF16)
    bd = C // LRU_BLOCKS
    r_parts, i_parts = [], []
    for hb in range(LRU_BLOCKS):
        cs = slice(hb * bd, (hb + 1) * bd)
        r_parts.append(jnp.dot(xcb[:, cs], wa_ref[hb], preferred_element_type=F32))
        i_parts.append(jnp.dot(xcb[:, cs], wi_ref[hb], preferred_element_type=F32))
    r = jax.nn.sigmoid(jnp.concatenate(r_parts, axis=1) + ba_ref[...])
    ig = jax.nn.sigmoid(jnp.concatenate(i_parts, axis=1) + bi_ref[...])
    log_a = -LRU_C * r * softplus_neg
    a = jnp.exp(log_a)
    y = jnp.maximum(-jnp.tanh(log_a) * (a * a + 1.0), 0.0)
    mult = jnp.where(y > 0.0, y * lax.rsqrt(y), 0.0)
    u = xc * ig * mult

    a3 = a.reshape(rows // SUBLANES, SUBLANES, C)
    b3 = u.reshape(rows // SUBLANES, SUBLANES, C)
    row = lax.broadcasted_iota(jnp.int32, (1, SUBLANES, C), 1)
    s = 1
    while s < SUBLANES:
        keep = row >= s
        a_sh = pltpu.roll(a3, s, axis=1)
        b_sh = pltpu.roll(b3, s, axis=1)
        b3 = jnp.where(keep, b3 + a3 * b_sh, b3)
        a3 = jnp.where(keep, a3 * a_sh, a3)
        s *= 2
    a_ref[...] = a3.reshape(rows, C)
    b_ref[...] = b3.reshape(rows, C)

    def grp(i, hprev):
        r0 = pl.multiple_of(i * SUBLANES, SUBLANES)
        hh = b_ref[pl.ds(r0, SUBLANES), :] + a_ref[pl.ds(r0, SUBLANES), :] * hprev
        b_ref[pl.ds(r0, SUBLANES), :] = hh
        return jnp.broadcast_to(hh[SUBLANES - 1:SUBLANES, :], (SUBLANES, C))

    hlast = lax.fori_loop(0, rows // SUBLANES, grp, jnp.broadcast_to(h_ref[...], (SUBLANES, C)))
    h_ref[...] = hlast[0:1, :]
    o_ref[...] = (b_ref[...] * jax.nn.gelu(xg_ref[...].astype(F32))).astype(o_ref.dtype)


def _lru(proj, B, S, conv_w, conv_b, w_a, b_a, w_i, b_i, lam, col_r, col_g):
    T = B * S
    C = lam.shape[0]
    ts = min(512, S)
    nt = S // ts
    hb = ts // SUBLANES
    kern = functools.partial(_lru_kernel, rows=ts)
    vec = lambda a: a.reshape(1, C).astype(F32)
    const2 = lambda b, t: (0, 0)
    return pl.pallas_call(
        kern,
        grid=(B, nt),
        in_specs=[
            pl.BlockSpec((ts, C), lambda b, t: (b * nt + t, col_r)),
            pl.BlockSpec((SUBLANES, C),
                         lambda b, t: (jnp.maximum((b * nt + t) * hb - 1, 0), col_r)),
            pl.BlockSpec((ts, C), lambda b, t: (b * nt + t, col_g)),
            pl.BlockSpec((CONV_WIDTH, C), const2),
            pl.BlockSpec((1, C), const2),
            pl.BlockSpec((LRU_BLOCKS, C // LRU_BLOCKS, C // LRU_BLOCKS), lambda b, t: (0, 0, 0)),
            pl.BlockSpec((1, C), const2),
            pl.BlockSpec((LRU_BLOCKS, C // LRU_BLOCKS, C // LRU_BLOCKS), lambda b, t: (0, 0, 0)),
            pl.BlockSpec((1, C), const2),
            pl.BlockSpec((1, C), const2),
        ],
        out_specs=pl.BlockSpec((ts, C), lambda b, t: (b * nt + t, 0)),
        out_shape=jax.ShapeDtypeStruct((T, C), BF16),
        scratch_shapes=[
            pltpu.VMEM((ts + SUBLANES, C), F32),
            pltpu.VMEM((ts, C), F32),
            pltpu.VMEM((ts, C), F32),
            pltpu.VMEM((1, C), F32),
        ],
        compiler_params=_params("parallel", "arbitrary"),
        name="rglru",
    )(proj, proj, proj, conv_w.astype(F32), vec(conv_b), w_a.astype(BF16), vec(b_a),
      w_i.astype(BF16), vec(b_i), vec(lam))


def _merge_kernel(ya_ref, yb_ref, yc_ref, g0_ref, g1_ref, g2_ref, bg_ref, wb_ref, wo_ref, x_ref, o_ref):
    merged = None
    for k, (y_ref, g_ref) in enumerate(((ya_ref, g0_ref), (yb_ref, g1_ref), (yc_ref, g2_ref))):
        gate = jax.nn.sigmoid(g_ref[...].astype(F32) + bg_ref[k:k + 1, :])
        term = gate * jnp.dot(y_ref[...], wb_ref[k], preferred_element_type=F32)
        merged = term if merged is None else merged + term
    o_ref[...] = x_ref[...] + jnp.dot(merged.astype(BF16), wo_ref[...], preferred_element_type=F32)


def _merge(ya, yb, yc, proj, col_gate, b_gate, w_branch, w_out, x2):
    T, D = x2.shape
    tm = min(512, T)
    tok = lambda i: (i, 0)
    return pl.pallas_call(
        _merge_kernel,
        grid=(T // tm,),
        in_specs=[
            pl.BlockSpec((tm, D), tok), pl.BlockSpec((tm, D), tok), pl.BlockSpec((tm, D), tok),
            pl.BlockSpec((tm, D), lambda i: (i, col_gate)),
            pl.BlockSpec((tm, D), lambda i: (i, col_gate + 1)),
            pl.BlockSpec((tm, D), lambda i: (i, col_gate + 2)),
            pl.BlockSpec((N_BRANCH, D), lambda i: (0, 0)),
            pl.BlockSpec((N_BRANCH, D, D), lambda i: (0, 0, 0)),
            pl.BlockSpec((D, D), lambda i: (0, 0)),
            pl.BlockSpec((tm, D), tok),
        ],
        out_specs=pl.BlockSpec((tm, D), tok),
        out_shape=jax.ShapeDtypeStruct((T, D), F32),
        compiler_params=_params("parallel"),
        name="merge",
    )(ya, yb, yc, proj, proj, proj, b_gate.astype(F32), w_branch, w_out, x2)


def _mlp_kernel(x_ref, g_ref, wu_ref, wd_ref, gf_ref, o_ref, *, ft, final_norm):
    x = x_ref[...]
    h = _rms(x, g_ref[...]).astype(BF16)
    acc = x
    for f in range(wu_ref.shape[1] // ft):
        hid = jnp.dot(h, wu_ref[:, f * ft:(f + 1) * ft], preferred_element_type=F32)
        hid = jnp.square(jnp.maximum(hid, 0.0)).astype(BF16)
        acc = acc + jnp.dot(hid, wd_ref[f * ft:(f + 1) * ft, :], preferred_element_type=F32)
    if final_norm:
        acc = _rms(acc, gf_ref[...])
    o_ref[...] = acc


def _mlp(x2, g, w_up, w_down, g_final, final_norm):
    T, D = x2.shape
    F = w_up.shape[1]
    tm = min(512, T)
    kern = functools.partial(_mlp_kernel, ft=1024, final_norm=final_norm)
    return pl.pallas_call(
        kern,
        grid=(T // tm,),
        in_specs=[
            pl.BlockSpec((tm, D), lambda i: (i, 0)),
            pl.BlockSpec((1, D), lambda i: (0, 0)),
            pl.BlockSpec((D, F), lambda i: (0, 0)),
            pl.BlockSpec((F, D), lambda i: (0, 0)),
            pl.BlockSpec((1, D), lambda i: (0, 0)),
        ],
        out_specs=pl.BlockSpec((tm, D), lambda i: (i, 0)),
        out_shape=jax.ShapeDtypeStruct((T, D), F32),
        compiler_params=_params("parallel"),
        name="mlp",
    )(x2, g.reshape(1, D).astype(F32), w_up, w_down, g_final.reshape(1, D).astype(F32))


def kernel(x, w_in, g_mix, da_lambda, da_subln_g, rel_bias, sg_ln_g, sg_ln_b, sg_w, sg_b, lru_conv_w, lru_conv_b, lru_w_a, lru_b_a, lru_w_i, lru_b_i, lru_lambda, b_gate, w_branch, w_out, g_mlp, w_up, w_down, g_final):
    B, S, D = x.shape
    T = B * S
    depth = w_in.shape[0]
    x2 = x.reshape(T, D).astype(F32)

    col_u, col_v, col_r, col_g, col_gate = 3, 4, 5, 6, 7
    qscale = jnp.concatenate([jnp.full((D,), DA_HEAD_DIM ** -0.5 * LOG2E, F32),
                              jnp.ones((w_in.shape[2] - D,), F32)])
    bias_diag, bias_corner, cfar = _bias_tiles(rel_bias, ATT_TILE // 2)

    for l in range(depth):
        w_l = (w_in[l] * qscale[None, :]).astype(BF16)
        proj = _inproj(x2, g_mix[l].astype(F32), w_l)
        ya = _attention(proj, B, S, da_lambda[l], da_subln_g[l], bias_diag, bias_corner, cfar, l)
        yb = _sgu(proj, T, sg_ln_g[l], sg_ln_b[l], sg_w[l], sg_b[l], col_u, col_v)
        yc = _lru(proj, B, S, lru_conv_w[l], lru_conv_b[l], lru_w_a[l], lru_b_a[l],
                  lru_w_i[l], lru_b_i[l], lru_lambda[l], col_r, col_g)
        x2 = _merge(ya, yb, yc, proj, col_gate, b_gate[l], w_branch[l].astype(BF16),
                    w_out[l].astype(BF16), x2)
        x2 = _mlp(x2, g_mlp[l], w_up[l].astype(BF16), w_down[l].astype(BF16), g_final,
                  final_norm=(l == depth - 1))
    return x2.reshape(B, S, D).astype(x.dtype)
```

```python
import functools
import math

import jax
import jax.numpy as jnp
from jax import lax
from jax.experimental import pallas as pl
from jax.experimental.pallas import tpu as pltpu

EPS = 1e-6
CHUNK = 64
DA_HEADS = 8
DA_HEAD_DIM = 64
DA_V_DIM = 2 * DA_HEAD_DIM
REL_BUCKETS = 32
REL_MAX_DIST = 128
SG_BLOCK = 128
SG_GROUPS = 4
LRU_BLOCKS = 8
CONV_WIDTH = 4
LRU_C = 8.0
N_BRANCH = 3

LANES = 128
SUBLANES = 8
VMEM_LIMIT = 56 * 1024 * 1024
NEG = -1e30
LOG2E = math.log2(math.e)

FRONT_ROWS = 512
ATT_TILE = 512
BF16 = jnp.bfloat16
F32 = jnp.float32


def _params(*sem):
    return pltpu.CompilerParams(dimension_semantics=sem, vmem_limit_bytes=VMEM_LIMIT)


def _rms(x, g):
    ms = jnp.mean(x * x, axis=-1, keepdims=True)
    return x * lax.rsqrt(ms + EPS) * g


COL_Q, COL_U, COL_V, COL_XR, COL_XG, COL_GATE = 0, 3, 4, 5, 6, 7
MXU_COLS = 256
LRU_UNIT_ROWS = 128


def _front_kernel(x_ref, g_ref, w_ref, lng_ref, lnb_ref, ws_ref, bs_ref,
                  cw_ref, cb_ref, wa_ref, ba_ref, wi_ref, bi_ref, lam_ref,
                  qkv_ref, gates_ref, yb_ref, yc_ref,
                  h_ref, ext_ref, a_ref, b_ref, hcar_ref, *, rows, tiles_per_seq):
    t = pl.program_id(0) % tiles_per_seq
    D = x_ref.shape[1]
    halo = SUBLANES
    nt = MXU_COLS
    n_tiles = D // nt
    ru = LRU_UNIT_ROWS
    h_ref[...] = _rms(x_ref[...], g_ref[...]).astype(BF16)

    def piece(block, n):
        c0 = block * D + n * nt
        return jnp.dot(h_ref[...], w_ref[:, c0:c0 + nt], preferred_element_type=F32)

    def store_piece(dst_ref, dst_block, block, n):
        dst_ref[:, dst_block * D + n * nt:dst_block * D + (n + 1) * nt] = (
            piece(block, n).astype(dst_ref.dtype))

    def xr_piece(n):
        ext_ref[halo:halo + rows, n * nt:(n + 1) * nt] = piece(COL_XR, n)

    def gate_piece(n):
        yc_ref[:, n * nt:(n + 1) * nt] = jax.nn.gelu(piece(COL_XG, n)).astype(yc_ref.dtype)

    pieces = ([functools.partial(xr_piece, n) for n in range(1, n_tiles)]
              + [functools.partial(gate_piece, n) for n in range(n_tiles)]
              + [functools.partial(store_piece, qkv_ref, c, COL_Q + c, n)
                 for c in range(3) for n in range(n_tiles)]
              + [functools.partial(store_piece, gates_ref, c, COL_GATE + c, n)
                 for c in range(N_BRANCH) for n in range(n_tiles)])

    def spend(k=1):
        for _ in range(k):
            if pieces:
                pieces.pop(0)()

    @pl.when(t == 0)
    def _():
        hcar_ref[...] = jnp.zeros(hcar_ref.shape, F32)
        ext_ref[0:halo, :] = jnp.zeros((halo, D), F32)

    @pl.when(t != 0)
    def _():
        ext_ref[0:halo, :] = ext_ref[rows:rows + halo, :]

    bd = D // LRU_BLOCKS
    row = lax.broadcasted_iota(jnp.int32, (1, SUBLANES, nt), 1)

    def lru_unit(n, rb):
        cs = slice(n * nt, (n + 1) * nt)
        r0 = rb * ru
        xc = jnp.zeros((ru, nt), F32) + cb_ref[:, cs]
        for j in range(CONV_WIDTH):
            off = halo - (CONV_WIDTH - 1) + j + r0
            xc = xc + cw_ref[j:j + 1, cs] * ext_ref[off:off + ru, cs]
        nz = -lam_ref[:, cs]
        softplus_neg = jnp.maximum(nz, 0.0) + jnp.log1p(jnp.exp(-jnp.abs(nz)))
        xcb = xc.astype(BF16)
        r_parts, i_parts = [], []
        for hb in range(n * nt // bd, (n + 1) * nt // bd):
            xs = xcb[:, hb * bd - n * nt:(hb + 1) * bd - n * nt]
            r_parts.append(jnp.dot(xs, wa_ref[hb], preferred_element_type=F32))
            i_parts.append(jnp.dot(xs, wi_ref[hb], preferred_element_type=F32))
        r = jax.nn.sigmoid(jnp.concatenate(r_parts, axis=1) + ba_ref[:, cs])
        ig = jax.nn.sigmoid(jnp.concatenate(i_parts, axis=1) + bi_ref[:, cs])
        log_a = -LRU_C * r * softplus_neg
        a = jnp.exp(log_a)
        y = jnp.maximum(-jnp.tanh(log_a) * (a * a + 1.0), 0.0)
        u = xc * ig * jnp.where(y > 0.0, y * lax.rsqrt(y), 0.0)
        a3 = a.reshape(ru // SUBLANES, SUBLANES, nt)
        b3 = u.reshape(ru // SUBLANES, SUBLANES, nt)
        s = 1
        while s < SUBLANES:
            keep = row >= s
            a_sh = pltpu.roll(a3, s, axis=1)
            b_sh = pltpu.roll(b3, s, axis=1)
            b3 = jnp.where(keep, b3 + a3 * b_sh, b3)
            a3 = jnp.where(keep, a3 * a_sh, a3)
            s *= 2
        a_ref[r0:r0 + ru, cs] = a3.reshape(ru, nt)
        b_ref[r0:r0 + ru, cs] = b3.reshape(ru, nt)

    xr_piece(0)
    k = 0
    for n in range(n_tiles):
        for rb in range(rows // ru):
            spend(1 + k % 2)
            lru_unit(n, rb)
            k += 1

    hprev = jnp.broadcast_to(hcar_ref[...], (SUBLANES, D))
    grp_per_unit = ru // SUBLANES
    for i in range(rows // SUBLANES):
        r0 = i * SUBLANES
        hh = b_ref[r0:r0 + SUBLANES, :] + a_ref[r0:r0 + SUBLANES, :] * hprev
        b_ref[r0:r0 + SUBLANES, :] = hh
        hprev = jnp.broadcast_to(hh[SUBLANES - 1:SUBLANES, :], (SUBLANES, D))
        if (i + 1) % grp_per_unit == 0:
            rs = slice(r0 + SUBLANES - ru, r0 + SUBLANES)
            yc_ref[rs, :] = (b_ref[rs, :] * yc_ref[rs, :].astype(F32)).astype(yc_ref.dtype)
            spend()
    hcar_ref[...] = hprev[0:1, :]

    for n in range(n_tiles):
        a_ref[:, n * nt:(n + 1) * nt] = jax.nn.gelu(piece(COL_V, n))
    spend()
    v = a_ref[...]
    mu = jnp.mean(v, axis=-1, keepdims=True)
    vc = v - mu
    var = jnp.mean(vc * vc, axis=-1, keepdims=True)
    vn = (vc * lax.rsqrt(var + EPS) * lng_ref[...] + lnb_ref[...]).astype(BF16)
    pos_q = lax.broadcasted_iota(jnp.int32, (SG_BLOCK, SG_BLOCK), 0)
    pos_k = lax.broadcasted_iota(jnp.int32, (SG_BLOCK, SG_BLOCK), 1)
    mask = (pos_k // CHUNK) <= (pos_q // CHUNK)
    gd = D // SG_GROUPS
    for g in range(SG_GROUPS):
        spend()
        w = jnp.where(mask, ws_ref[g], 0.0).astype(BF16)
        cs = slice(g * gd, (g + 1) * gd)
        ug = jax.nn.gelu(jnp.dot(h_ref[...], w_ref[:, COL_U * D + g * gd:COL_U * D + (g + 1) * gd],
                                 preferred_element_type=F32))
        for nb in range(rows // SG_BLOCK):
            rs = slice(nb * SG_BLOCK, (nb + 1) * SG_BLOCK)
            mixed = jnp.dot(w, vn[rs, cs], preferred_element_type=F32) + bs_ref[:, cs]
            yb_ref[rs, cs] = (ug[rs, :] * mixed).astype(yb_ref.dtype)
    spend(len(pieces))


def _front(x2, S, g, w, ln_g, ln_b, w_s, b_s, conv_w, conv_b, w_a, b_a, w_i, b_i, lam):
    T, D = x2.shape
    N = w.shape[1]
    tm = min(FRONT_ROWS, S)
    gd = D // SG_GROUPS
    bsb = jnp.repeat(b_s.T.astype(F32), gd, axis=1)
    vec = lambda a: a.reshape(1, D).astype(F32)
    kern = functools.partial(_front_kernel, rows=tm, tiles_per_seq=S // tm)
    const2 = lambda i: (0, 0)
    const3 = lambda i: (0, 0, 0)
    once = pl.Buffered(1)
    nb = D // LRU_BLOCKS
    tok = lambda i: (i, 0)
    return pl.pallas_call(
        kern,
        grid=(T // tm,),
        in_specs=[
            pl.BlockSpec((tm, D), tok),
            pl.BlockSpec((1, D), const2),
            pl.BlockSpec((D, N), const2, pipeline_mode=once),
            pl.BlockSpec((1, D), const2), pl.BlockSpec((1, D), const2),
            pl.BlockSpec((SG_GROUPS, SG_BLOCK, SG_BLOCK), const3),
            pl.BlockSpec((SG_BLOCK, D), const2),
            pl.BlockSpec((CONV_WIDTH, D), const2), pl.BlockSpec((1, D), const2),
            pl.BlockSpec((LRU_BLOCKS, nb, nb), const3), pl.BlockSpec((1, D), const2),
            pl.BlockSpec((LRU_BLOCKS, nb, nb), const3), pl.BlockSpec((1, D), const2),
            pl.BlockSpec((1, D), const2),
        ],
        out_specs=[pl.BlockSpec((tm, 3 * D), tok), pl.BlockSpec((tm, 3 * D), tok),
                   pl.BlockSpec((tm, D), tok), pl.BlockSpec((tm, D), tok)],
        out_shape=[jax.ShapeDtypeStruct((T, 3 * D), BF16), jax.ShapeDtypeStruct((T, 3 * D), BF16),
                   jax.ShapeDtypeStruct((T, D), BF16), jax.ShapeDtypeStruct((T, D), BF16)],
        scratch_shapes=[
            pltpu.VMEM((tm, D), BF16),
            pltpu.VMEM((tm + SUBLANES, D), F32),
            pltpu.VMEM((tm, D), F32), pltpu.VMEM((tm, D), F32),
            pltpu.VMEM((1, D), F32),
        ],
        compiler_params=_params("arbitrary"),
        name="front",
    )(x2, vec(g), w, vec(ln_g), vec(ln_b), w_s.astype(F32), bsb,
      conv_w.astype(F32), vec(conv_b), w_a.astype(BF16), vec(b_a), w_i.astype(BF16), vec(b_i),
      vec(lam))


def _attn_kernel(cfar_ref, lp_ref, q_ref, k_ref, v_ref, dblk_ref, corner_ref, g_ref, o_ref,
                 k1_ref, k2_ref, vt_ref, qt_ref, sa_ref, sb_ref, sc_ref, mca_ref, mcb_ref, mcc_ref,
                 m_ref, l_ref, a_ref,
                 *, seq, tile, lam_init):
    h = pl.program_id(0)
    nq = seq // tile
    grp = tile // SUBLANES
    lane = lax.broadcasted_iota(jnp.int32, (1, LANES), 1)
    lo_half = lane < DA_HEAD_DIM

    c = jnp.full((1, LANES), cfar_ref[h], F32)
    c_hi = c.astype(BF16).astype(F32)
    c_lo = c - c_hi
    kfill1 = jnp.where(lane == DA_HEAD_DIM, c_hi, jnp.where(lane == DA_HEAD_DIM + 1, c_lo, 0.0))
    kfill2 = jnp.where(lane == 0, c_hi, jnp.where(lane == 1, c_lo, 0.0))
    qfill1 = jnp.where((lane == DA_HEAD_DIM) | (lane == DA_HEAD_DIM + 1), 1.0, 0.0)
    qfill2 = jnp.where((lane == 0) | (lane == 1), 1.0, 0.0)

    kk = k_ref[...].astype(F32)
    k1_ref[...] = jnp.where(lo_half, kk, kfill1).astype(BF16)
    k2_ref[...] = jnp.where(lo_half, kfill2, kk).astype(BF16)
    for c0 in range(0, seq, tile):
        vt_ref[:, c0:c0 + tile] = v_ref[c0:c0 + tile, :].astype(F32).T.astype(BF16)

    lp = lp_ref[...]
    lam = (jnp.exp(jnp.sum(lp[0:1] * lp[1:2], axis=-1, keepdims=True))
           - jnp.exp(jnp.sum(lp[2:3] * lp[3:4], axis=-1, keepdims=True)) + lam_init)

    def load_queries(i):
        q0 = pl.multiple_of(i * tile, tile)
        qq = q_ref[pl.ds(q0, tile), :].astype(F32)
        qt_ref[i % 2, 0] = jnp.where(lo_half, qq, qfill1).T.astype(BF16)
        qt_ref[i % 2, 1] = jnp.where(lo_half, qfill2, qq).T.astype(BF16)

    def scores(i, j, dst):
        s_ref, mc_ref = dst
        k0 = pl.multiple_of(j * tile, tile)
        for m, ka_ref in enumerate((k1_ref, k2_ref)):
            s = jnp.dot(ka_ref[pl.ds(k0, tile), :], qt_ref[i % 2, m], preferred_element_type=F32)
            s_ref[m] = s
            mc_ref[m] = jnp.max(s.reshape(grp, SUBLANES, tile), axis=0)

    half = tile // 2

    def scores_diag(i, dst):
        s_ref, _ = dst
        k0 = pl.multiple_of(i * tile, tile)
        k1 = pl.multiple_of(i * tile + half, half)
        for m, ka_ref in enumerate((k1_ref, k2_ref)):
            s_ref[m, 0:half, :] = jnp.dot(ka_ref[pl.ds(k0, half), :], qt_ref[i % 2, m],
                                          preferred_element_type=F32)
            s_ref[m, half:tile, half:tile] = jnp.dot(ka_ref[pl.ds(k1, half), :],
                                                     qt_ref[i % 2, m, :, half:tile],
                                                     preferred_element_type=F32)

    def softmax_diag(i, src):
        s_ref, _ = src
        k0 = pl.multiple_of(i * tile, tile)
        k1 = pl.multiple_of(i * tile + half, half)
        hgrp = half // SUBLANES
        colmax = lambda x: jnp.max(x.reshape(hgrp, SUBLANES, half), axis=0)
        colsum = lambda x: jnp.sum(x.reshape(hgrp, SUBLANES, half), axis=0)
        dblk = dblk_ref[...]
        vt_lo = vt_ref[:, pl.ds(k0, half)]
        vt_hi = vt_ref[:, pl.ds(k1, half)]
        for m in range(2):
            s_lo = s_ref[m, 0:half, 0:half] + dblk
            m_prev = m_ref[m, :, 0:half]
            m_new = jnp.maximum(m_prev, jnp.max(colmax(s_lo), axis=0, keepdims=True))
            alpha = jnp.exp2(m_prev - m_new)
            p_lo = jnp.exp2(s_lo - m_new)
            l_ref[m, :, 0:half] = alpha * l_ref[m, :, 0:half] + colsum(p_lo)
            a_ref[m, :, 0:half] = alpha * a_ref[m, :, 0:half] + jnp.dot(
                vt_lo, p_lo.astype(BF16), preferred_element_type=F32)
            m_ref[m, :, 0:half] = m_new
            s_ref[m, half - LANES:half, half:half + LANES] = (
                s_ref[m, half - LANES:half, half:half + LANES] + corner_ref[...])
            s_top = s_ref[m, 0:half, half:tile]
            s_bot = s_ref[m, half:tile, half:tile] + dblk
            m_prev = m_ref[m, :, half:tile]
            m_cur = jnp.maximum(colmax(s_top), colmax(s_bot))
            m_new = jnp.maximum(m_prev, jnp.max(m_cur, axis=0, keepdims=True))
            alpha = jnp.exp2(m_prev - m_new)
            p_top = jnp.exp2(s_top - m_new)
            p_bot = jnp.exp2(s_bot - m_new)
            l_ref[m, :, half:tile] = alpha * l_ref[m, :, half:tile] + colsum(p_top) + colsum(p_bot)
            a_ref[m, :, half:tile] = (alpha * a_ref[m, :, half:tile]
                                      + jnp.dot(vt_lo, p_top.astype(BF16), preferred_element_type=F32)
                                      + jnp.dot(vt_hi, p_bot.astype(BF16), preferred_element_type=F32))
            m_ref[m, :, half:tile] = m_new

    def near_fix(dst):
        s_ref, mc_ref = dst
        for m in range(2):
            s_ref[m, tile - LANES:tile, 0:LANES] = (s_ref[m, tile - LANES:tile, 0:LANES]
                                                    + corner_ref[...])
            col = s_ref[m, :, 0:LANES]
            mc_ref[m, :, 0:LANES] = jnp.max(col.reshape(grp, SUBLANES, LANES), axis=0)

    def softmax_pv(j, src):
        s_ref, mc_ref = src
        k0 = pl.multiple_of(j * tile, tile)
        vt = vt_ref[:, pl.ds(k0, tile)]
        for m in range(2):
            s = s_ref[m]
            m_prev = m_ref[m]
            m_new = jnp.maximum(m_prev, jnp.max(mc_ref[m], axis=0, keepdims=True))
            alpha = jnp.exp2(m_prev - m_new)
            p = jnp.exp2(s - m_new)
            l_ref[m] = alpha * l_ref[m] + jnp.sum(p.reshape(grp, SUBLANES, tile), axis=0)
            a_ref[m] = alpha * a_ref[m] + jnp.dot(vt, p.astype(BF16), preferred_element_type=F32)
            m_ref[m] = m_new

    sa = (sa_ref, mca_ref)
    sb = (sb_ref, mcb_ref)
    sc = (sc_ref, mcc_ref)

    load_queries(0)
    scores_diag(0, sa)

    def q_body(i, carry):
        m_ref[...] = jnp.full(m_ref.shape, NEG, F32)
        l_ref[...] = jnp.zeros(l_ref.shape, F32)
        a_ref[...] = jnp.zeros(a_ref.shape, F32)
        even = (i % 2) == 0
        nxt = jnp.minimum(i + 1, nq - 1)

        @pl.when(even & (i >= 2))
        def _():
            scores(i, 1, sb)
            softmax_pv(0, sc)

        j_first = jnp.where(even, 1, 0)

        def far_pair(t, c2):
            j = j_first + 2 * t
            scores(i, j + 1, sa)
            softmax_pv(j, sb)
            scores(i, j + 2, sb)
            softmax_pv(j + 1, sa)
            return c2

        lax.fori_loop(0, jnp.maximum(i - 1, 0) // 2, far_pair, 0)

        @pl.when(i >= 1)
        def _():
            scores_diag(i, sa)
            near_fix(sb)
            softmax_pv(i - 1, sb)

        @pl.when(even)
        def _():
            load_queries(nxt)
            scores(nxt, 0, sb)
            softmax_diag(i, sa)

        @pl.when(jnp.logical_not(even))
        def _():
            load_queries(nxt)
            scores(nxt, 0, sc)
            softmax_diag(i, sa)

        r1 = 1.0 / jnp.sum(l_ref[0], axis=0, keepdims=True)
        r2 = 1.0 / jnp.sum(l_ref[1], axis=0, keepdims=True)
        ot = a_ref[0] * r1 - a_ref[1] * (lam * r2)
        ms = jnp.mean(ot * ot, axis=0, keepdims=True)
        y = (ot * lax.rsqrt(ms + EPS)).T * g_ref[...] * (1.0 - lam_init)
        q0 = pl.multiple_of(i * tile, tile)
        o_ref[pl.ds(q0, tile), :] = y.astype(o_ref.dtype)
        return carry

    lax.fori_loop(0, nq, q_body, 0)


def _t5_bucket(rel):
    nb = REL_BUCKETS // 2
    max_exact = nb // 2
    ret = (rel > 0).astype(jnp.int32) * nb
    n = jnp.abs(rel)
    nf = jnp.maximum(n, 1).astype(jnp.float32)
    large = max_exact + (jnp.log(nf / max_exact) / math.log(REL_MAX_DIST / max_exact)
                         * (nb - max_exact)).astype(jnp.int32)
    large = jnp.minimum(large, nb - 1)
    return ret + jnp.where(n < max_exact, n, large)


def _bias_tiles(rel_table, block):
    table = rel_table.astype(F32) * LOG2E
    cfar = table[_t5_bucket(jnp.int32(-REL_MAX_DIST))]
    pos = jnp.arange(block, dtype=jnp.int32)
    rel = pos[None, :] - pos[:, None]
    allowed = (pos[None, :] // CHUNK) <= (pos[:, None] // CHUNK)
    diag = table[_t5_bucket(rel)].transpose(2, 0, 1) - cfar[:, None, None]
    diag = jnp.where(allowed[None], diag, NEG).transpose(0, 2, 1)
    cpos = jnp.arange(LANES, dtype=jnp.int32)
    crel = (cpos[None, :] - LANES) - cpos[:, None]
    corner = (table[_t5_bucket(crel)].transpose(2, 0, 1) - cfar[:, None, None]).transpose(0, 2, 1)
    return diag, corner, cfar


def _attention(qkv, B, S, lam_params, subln_g, bias_diag, bias_corner, cfar, layer_idx):
    T = B * S
    tile = ATT_TILE
    assert tile // 2 >= LANES >= REL_MAX_DIST and S % tile == 0
    lam_init = 0.8 - 0.6 * math.exp(-0.3 * layer_idx)
    kern = functools.partial(_attn_kernel, seq=S, tile=tile, lam_init=lam_init)
    H = DA_HEADS
    return pl.pallas_call(
        kern,
        grid=(H, B),
        in_specs=[
            pl.BlockSpec(memory_space=pltpu.SMEM),
            pl.BlockSpec((4, DA_HEAD_DIM), lambda h, b: (0, 0)),
            pl.BlockSpec((S, LANES), lambda h, b: (b, h)),
            pl.BlockSpec((S, LANES), lambda h, b: (b, H + h)),
            pl.BlockSpec((S, LANES), lambda h, b: (b, 2 * H + h)),
            pl.BlockSpec((None, tile // 2, tile // 2), lambda h, b: (h, 0, 0)),
            pl.BlockSpec((None, LANES, LANES), lambda h, b: (h, 0, 0)),
            pl.BlockSpec((1, DA_V_DIM), lambda h, b: (0, 0)),
        ],
        out_specs=pl.BlockSpec((S, LANES), lambda h, b: (b, h)),
        out_shape=jax.ShapeDtypeStruct((T, H * DA_V_DIM), BF16),
        scratch_shapes=[
            pltpu.VMEM((S, LANES), BF16), pltpu.VMEM((S, LANES), BF16), pltpu.VMEM((DA_V_DIM, S), BF16),
            pltpu.VMEM((2, 2, LANES, tile), BF16),
            pltpu.VMEM((2, tile, tile), F32), pltpu.VMEM((2, tile, tile), F32),
            pltpu.VMEM((2, tile, tile), F32),
            pltpu.VMEM((2, SUBLANES, tile), F32), pltpu.VMEM((2, SUBLANES, tile), F32),
            pltpu.VMEM((2, SUBLANES, tile), F32),
            pltpu.VMEM((2, 1, tile), F32), pltpu.VMEM((2, SUBLANES, tile), F32),
            pltpu.VMEM((2, DA_V_DIM, tile), F32),
        ],
        compiler_params=_params("parallel", "parallel"),
        name="diff_attn",
    )(cfar, lam_params.astype(F32), qkv, qkv, qkv, bias_diag, bias_corner,
      subln_g.reshape(1, DA_V_DIM).astype(F32))


def _merge_kernel(ya_ref, yb_ref, yc_ref, g0_ref, g1_ref, g2_ref, bg_ref, wb_ref, wo_ref, x_ref, o_ref):
    merged = None
    for k, (y_ref, g_ref) in enumerate(((ya_ref, g0_ref), (yb_ref, g1_ref), (yc_ref, g2_ref))):
        gate = jax.nn.sigmoid(g_ref[...].astype(F32) + bg_ref[k:k + 1, :])
        term = gate * jnp.dot(y_ref[...], wb_ref[k], preferred_element_type=F32)
        merged = term if merged is None else merged + term
    o_ref[...] = x_ref[...] + jnp.dot(merged.astype(BF16), wo_ref[...], preferred_element_type=F32)


def _merge(ya, yb, yc, gates, b_gate, w_branch, w_out, x2):
    T, D = x2.shape
    tm = min(512, T)
    tok = lambda i: (i, 0)
    return pl.pallas_call(
        _merge_kernel,
        grid=(T // tm,),
        in_specs=[
            pl.BlockSpec((tm, D), tok), pl.BlockSpec((tm, D), tok), pl.BlockSpec((tm, D), tok),
            pl.BlockSpec((tm, D), lambda i: (i, 0)),
            pl.BlockSpec((tm, D), lambda i: (i, 1)),
            pl.BlockSpec((tm, D), lambda i: (i, 2)),
            pl.BlockSpec((N_BRANCH, D), lambda i: (0, 0)),
            pl.BlockSpec((N_BRANCH, D, D), lambda i: (0, 0, 0)),
            pl.BlockSpec((D, D), lambda i: (0, 0)),
            pl.BlockSpec((tm, D), tok),
        ],
        out_specs=pl.BlockSpec((tm, D), tok),
        out_shape=jax.ShapeDtypeStruct((T, D), F32),
        compiler_params=_params("parallel"),
        name="merge",
    )(ya, yb, yc, gates, gates, gates, b_gate.astype(F32), w_branch, w_out, x2)


def _mlp_kernel(x_ref, g_ref, wu_ref, wd_ref, gf_ref, o_ref, *, ft, final_norm):
    x = x_ref[...]
    h = _rms(x, g_ref[...]).astype(BF16)
    acc = x
    for f in range(wu_ref.shape[1] // ft):
        hid = jnp.dot(h, wu_ref[:, f * ft:(f + 1) * ft], preferred_element_type=F32)
        hid = jnp.square(jnp.maximum(hid, 0.0)).astype(BF16)
        acc = acc + jnp.dot(hid, wd_ref[f * ft:(f + 1) * ft, :], preferred_element_type=F32)
    if final_norm:
        acc = _rms(acc, gf_ref[...])
    o_ref[...] = acc


def _mlp(x2, g, w_up, w_down, g_final, final_norm):
    T, D = x2.shape
    F = w_up.shape[1]
    tm = min(512, T)
    kern = functools.partial(_mlp_kernel, ft=1024, final_norm=final_norm)
    return pl.pallas_call(
        kern,
        grid=(T // tm,),
        in_specs=[
            pl.BlockSpec((tm, D), lambda i: (i, 0)),
            pl.BlockSpec((1, D), lambda i: (0, 0)),
            pl.BlockSpec((D, F), lambda i: (0, 0)),
            pl.BlockSpec((F, D), lambda i: (0, 0)),
            pl.BlockSpec((1, D), lambda i: (0, 0)),
        ],
        out_specs=pl.BlockSpec((tm, D), lambda i: (i, 0)),
        out_shape=jax.ShapeDtypeStruct((T, D), F32),
        compiler_params=_params("parallel"),
        name="mlp",
    )(x2, g.reshape(1, D).astype(F32), w_up, w_down, g_final.reshape(1, D).astype(F32))


def kernel(x, w_in, g_mix, da_lambda, da_subln_g, rel_bias, sg_ln_g, sg_ln_b, sg_w, sg_b, lru_conv_w, lru_conv_b, lru_w_a, lru_b_a, lru_w_i, lru_b_i, lru_lambda, b_gate, w_branch, w_out, g_mlp, w_up, w_down, g_final):
    B, S, D = x.shape
    T = B * S
    depth = w_in.shape[0]
    x2 = x.reshape(T, D).astype(F32)

    qscale = jnp.concatenate([jnp.full((D,), DA_HEAD_DIM ** -0.5 * LOG2E, F32),
                              jnp.ones((w_in.shape[2] - D,), F32)])
    bias_diag, bias_corner, cfar = _bias_tiles(rel_bias, ATT_TILE // 2)

    for l in range(depth):
        w_l = (w_in[l] * qscale[None, :]).astype(BF16)
        qkv, gates, yb, yc = _front(x2, S, g_mix[l], w_l, sg_ln_g[l], sg_ln_b[l], sg_w[l], sg_b[l],
                                    lru_conv_w[l], lru_conv_b[l], lru_w_a[l], lru_b_a[l],
                                    lru_w_i[l], lru_b_i[l], lru_lambda[l])
        ya = _attention(qkv, B, S, da_lambda[l], da_subln_g[l], bias_diag, bias_corner, cfar, l)
        x2 = _merge(ya, yb, yc, gates, b_gate[l], w_branch[l].astype(BF16),
                    w_out[l].astype(BF16), x2)
        x2 = _mlp(x2, g_mlp[l], w_up[l].astype(BF16), w_down[l].astype(BF16), g_final,
                  final_norm=(l == depth - 1))
    return x2.reshape(B, S, D).astype(x.dtype)
```

```python
import functools
import math

import jax
import jax.numpy as jnp
from jax import lax
from jax.experimental import pallas as pl
from jax.experimental.pallas import tpu as pltpu

EPS = 1e-6
CHUNK = 64
DA_HEADS = 8
DA_HEAD_DIM = 64
DA_V_DIM = 2 * DA_HEAD_DIM
REL_BUCKETS = 32
REL_MAX_DIST = 128
SG_BLOCK = 128
SG_GROUPS = 4
LRU_BLOCKS = 8
CONV_WIDTH = 4
LRU_C = 8.0
N_BRANCH = 3

LANES = 128
SUBLANES = 8
VMEM_LIMIT = 56 * 1024 * 1024
NEG = -1e30
LOG2E = math.log2(math.e)

FRONT_ROWS = 512
ATT_TILE = 512
BF16 = jnp.bfloat16
F32 = jnp.float32


def _params(*sem):
    return pltpu.CompilerParams(dimension_semantics=sem, vmem_limit_bytes=VMEM_LIMIT)


def _rms(x, g):
    ms = jnp.mean(x * x, axis=-1, keepdims=True)
    return x * lax.rsqrt(ms + EPS) * g


COL_Q, COL_U, COL_V, COL_XR, COL_XG, COL_GATE = 0, 3, 4, 5, 6, 7
MXU_COLS = 256
LRU_UNIT_ROWS = 128


def _front_kernel(x_ref, g_ref, w_ref, lng_ref, lnb_ref, ws_ref, bs_ref,
                  cw_ref, cb_ref, wa_ref, ba_ref, wi_ref, bi_ref, lam_ref,
                  qkv_ref, gates_ref, yb_ref, yc_ref,
                  h_ref, ext_ref, a_ref, b_ref, v_ref, hcar_ref, *, rows, tiles_per_seq):
    t = pl.program_id(0) % tiles_per_seq
    D = x_ref.shape[1]
    halo = SUBLANES
    nt = MXU_COLS
    n_tiles = D // nt
    ru = LRU_UNIT_ROWS
    h_ref[...] = _rms(x_ref[...], g_ref[...]).astype(BF16)

    def piece(block, n):
        c0 = block * D + n * nt
        return jnp.dot(h_ref[...], w_ref[:, c0:c0 + nt], preferred_element_type=F32)

    def store_piece(dst_ref, dst_block, block, n):
        dst_ref[:, dst_block * D + n * nt:dst_block * D + (n + 1) * nt] = (
            piece(block, n).astype(dst_ref.dtype))

    def xr_piece(n):
        ext_ref[halo:halo + rows, n * nt:(n + 1) * nt] = piece(COL_XR, n)

    def gate_piece(n):
        yc_ref[:, n * nt:(n + 1) * nt] = jax.nn.gelu(piece(COL_XG, n)).astype(yc_ref.dtype)

    def v_piece(n):
        v_ref[:, n * nt:(n + 1) * nt] = piece(COL_V, n)

    pieces = ([functools.partial(xr_piece, n) for n in range(1, n_tiles)]
              + [functools.partial(gate_piece, n) for n in range(n_tiles)]
              + [functools.partial(v_piece, n) for n in range(n_tiles)]
              + [functools.partial(store_piece, qkv_ref, c, COL_Q + c, n)
                 for c in range(3) for n in range(n_tiles)]
              + [functools.partial(store_piece, gates_ref, c, COL_GATE + c, n)
                 for c in range(N_BRANCH) for n in range(n_tiles)])

    def spend(k=1):
        for _ in range(k):
            if pieces:
                pieces.pop(0)()

    @pl.when(t == 0)
    def _():
        hcar_ref[...] = jnp.zeros(hcar_ref.shape, F32)
        ext_ref[0:halo, :] = jnp.zeros((halo, D), F32)

    @pl.when(t != 0)
    def _():
        ext_ref[0:halo, :] = ext_ref[rows:rows + halo, :]

    bd = D // LRU_BLOCKS
    row = lax.broadcasted_iota(jnp.int32, (1, SUBLANES, nt), 1)

    def lru_unit(n, rb):
        cs = slice(n * nt, (n + 1) * nt)
        r0 = rb * ru
        xc = jnp.zeros((ru, nt), F32) + cb_ref[:, cs]
        for j in range(CONV_WIDTH):
            off = halo - (CONV_WIDTH - 1) + j + r0
            xc = xc + cw_ref[j:j + 1, cs] * ext_ref[off:off + ru, cs]
        spend()
        nz = -lam_ref[:, cs]
        softplus_neg = jnp.maximum(nz, 0.0) + jnp.log1p(jnp.exp(-jnp.abs(nz)))
        xcb = xc.astype(BF16)
        r_parts, i_parts = [], []
        for hb in range(n * nt // bd, (n + 1) * nt // bd):
            xs = xcb[:, hb * bd - n * nt:(hb + 1) * bd - n * nt]
            r_parts.append(jnp.dot(xs, wa_ref[hb], preferred_element_type=F32))
            i_parts.append(jnp.dot(xs, wi_ref[hb], preferred_element_type=F32))
        r = jax.nn.sigmoid(jnp.concatenate(r_parts, axis=1) + ba_ref[:, cs])
        ig = jax.nn.sigmoid(jnp.concatenate(i_parts, axis=1) + bi_ref[:, cs])
        log_a = -LRU_C * r * softplus_neg
        a = jnp.exp(log_a)
        y = jnp.maximum(-jnp.tanh(log_a) * (a * a + 1.0), 0.0)
        u = xc * ig * jnp.where(y > 0.0, y * lax.rsqrt(y), 0.0)
        a3 = a.reshape(ru // SUBLANES, SUBLANES, nt)
        b3 = u.reshape(ru // SUBLANES, SUBLANES, nt)
        s = 1
        while s < SUBLANES:
            keep = row >= s
            a_sh = pltpu.roll(a3, s, axis=1)
            b_sh = pltpu.roll(b3, s, axis=1)
            b3 = jnp.where(keep, b3 + a3 * b_sh, b3)
            a3 = jnp.where(keep, a3 * a_sh, a3)
            s *= 2
        a_ref[r0:r0 + ru, cs] = a3.reshape(ru, nt)
        b_ref[r0:r0 + ru, cs] = b3.reshape(ru, nt)

    xr_piece(0)
    for n in range(n_tiles):
        for rb in range(rows // ru):
            spend()
            lru_unit(n, rb)

    def sgu_norm(rb):
        rs = slice(rb * ru, (rb + 1) * ru)
        v = jax.nn.gelu(v_ref[rs, :])
        mu = jnp.mean(v, axis=-1, keepdims=True)
        vc = v - mu
        var = jnp.mean(vc * vc, axis=-1, keepdims=True)
        v_ref[rs, :] = vc * lax.rsqrt(var + EPS) * lng_ref[...] + lnb_ref[...]

    hprev = jnp.broadcast_to(hcar_ref[...], (SUBLANES, D))
    grp_per_unit = ru // SUBLANES
    for i in range(rows // SUBLANES):
        r0 = i * SUBLANES
        hh = b_ref[r0:r0 + SUBLANES, :] + a_ref[r0:r0 + SUBLANES, :] * hprev
        b_ref[r0:r0 + SUBLANES, :] = hh
        hprev = jnp.broadcast_to(hh[SUBLANES - 1:SUBLANES, :], (SUBLANES, D))
        if (i + 1) % grp_per_unit == 0:
            rs = slice(r0 + SUBLANES - ru, r0 + SUBLANES)
            yc_ref[rs, :] = (b_ref[rs, :] * yc_ref[rs, :].astype(F32)).astype(yc_ref.dtype)
            sgu_norm(i // grp_per_unit)
            spend()
    hcar_ref[...] = hprev[0:1, :]

    vn = v_ref[...].astype(BF16)
    pos_q = lax.broadcasted_iota(jnp.int32, (SG_BLOCK, SG_BLOCK), 0)
    pos_k = lax.broadcasted_iota(jnp.int32, (SG_BLOCK, SG_BLOCK), 1)
    mask = (pos_k // CHUNK) <= (pos_q // CHUNK)
    gd = D // SG_GROUPS
    for g in range(SG_GROUPS):
        spend()
        w = jnp.where(mask, ws_ref[g], 0.0).astype(BF16)
        cs = slice(g * gd, (g + 1) * gd)
        ug = jax.nn.gelu(jnp.dot(h_ref[...], w_ref[:, COL_U * D + g * gd:COL_U * D + (g + 1) * gd],
                                 preferred_element_type=F32))
        for nb in range(rows // SG_BLOCK):
            rs = slice(nb * SG_BLOCK, (nb + 1) * SG_BLOCK)
            mixed = jnp.dot(w, vn[rs, cs], preferred_element_type=F32) + bs_ref[:, cs]
            yb_ref[rs, cs] = (ug[rs, :] * mixed).astype(yb_ref.dtype)
    spend(len(pieces))


def _front(x2, S, g, w, ln_g, ln_b, w_s, b_s, conv_w, conv_b, w_a, b_a, w_i, b_i, lam):
    T, D = x2.shape
    N = w.shape[1]
    tm = min(FRONT_ROWS, S)
    gd = D // SG_GROUPS
    bsb = jnp.repeat(b_s.T.astype(F32), gd, axis=1)
    vec = lambda a: a.reshape(1, D).astype(F32)
    kern = functools.partial(_front_kernel, rows=tm, tiles_per_seq=S // tm)
    const2 = lambda i: (0, 0)
    const3 = lambda i: (0, 0, 0)
    once = pl.Buffered(1)
    nb = D // LRU_BLOCKS
    tok = lambda i: (i, 0)
    return pl.pallas_call(
        kern,
        grid=(T // tm,),
        in_specs=[
            pl.BlockSpec((tm, D), tok),
            pl.BlockSpec((1, D), const2),
            pl.BlockSpec((D, N), const2, pipeline_mode=once),
            pl.BlockSpec((1, D), const2), pl.BlockSpec((1, D), const2),
            pl.BlockSpec((SG_GROUPS, SG_BLOCK, SG_BLOCK), const3),
            pl.BlockSpec((SG_BLOCK, D), const2),
            pl.BlockSpec((CONV_WIDTH, D), const2), pl.BlockSpec((1, D), const2),
            pl.BlockSpec((LRU_BLOCKS, nb, nb), const3), pl.BlockSpec((1, D), const2),
            pl.BlockSpec((LRU_BLOCKS, nb, nb), const3), pl.BlockSpec((1, D), const2),
            pl.BlockSpec((1, D), const2),
        ],
        out_specs=[pl.BlockSpec((tm, 3 * D), tok), pl.BlockSpec((tm, 3 * D), tok),
                   pl.BlockSpec((tm, D), tok), pl.BlockSpec((tm, D), tok)],
        out_shape=[jax.ShapeDtypeStruct((T, 3 * D), BF16), jax.ShapeDtypeStruct((T, 3 * D), BF16),
                   jax.ShapeDtypeStruct((T, D), BF16), jax.ShapeDtypeStruct((T, D), BF16)],
        scratch_shapes=[
            pltpu.VMEM((tm, D), BF16),
            pltpu.VMEM((tm + SUBLANES, D), F32),
            pltpu.VMEM((tm, D), F32), pltpu.VMEM((tm, D), F32), pltpu.VMEM((tm, D), F32),
            pltpu.VMEM((1, D), F32),
        ],
        compiler_params=_params("arbitrary"),
        name="front",
    )(x2, vec(g), w, vec(ln_g), vec(ln_b), w_s.astype(F32), bsb,
      conv_w.astype(F32), vec(conv_b), w_a.astype(BF16), vec(b_a), w_i.astype(BF16), vec(b_i),
      vec(lam))


def _attn_kernel(cfar_ref, lp_ref, q_ref, k_ref, v_ref, dblk_ref, corner_ref, g_ref, o_ref,
                 k1_ref, k2_ref, vt_ref, qt_ref, sa_ref, sb_ref, sc_ref, mca_ref, mcb_ref, mcc_ref,
                 m_ref, l_ref, a_ref,
                 *, seq, tile, lam_init):
    h = pl.program_id(0)
    nq = seq // tile
    grp = tile // SUBLANES
    lane = lax.broadcasted_iota(jnp.int32, (1, LANES), 1)
    lo_half = lane < DA_HEAD_DIM

    c = jnp.full((1, LANES), cfar_ref[h], F32)
    c_hi = c.astype(BF16).astype(F32)
    c_lo = c - c_hi
    kfill1 = jnp.where(lane == DA_HEAD_DIM, c_hi, jnp.where(lane == DA_HEAD_DIM + 1, c_lo, 0.0))
    kfill2 = jnp.where(lane == 0, c_hi, jnp.where(lane == 1, c_lo, 0.0))
    qfill1 = jnp.where((lane == DA_HEAD_DIM) | (lane == DA_HEAD_DIM + 1), 1.0, 0.0)
    qfill2 = jnp.where((lane == 0) | (lane == 1), 1.0, 0.0)

    kk = k_ref[...].astype(F32)
    k1_ref[...] = jnp.where(lo_half, kk, kfill1).astype(BF16)
    k2_ref[...] = jnp.where(lo_half, kfill2, kk).astype(BF16)
    for c0 in range(0, seq, tile):
        vt_ref[:, c0:c0 + tile] = v_ref[c0:c0 + tile, :].astype(F32).T.astype(BF16)

    lp = lp_ref[...]
    lam = (jnp.exp(jnp.sum(lp[0:1] * lp[1:2], axis=-1, keepdims=True))
           - jnp.exp(jnp.sum(lp[2:3] * lp[3:4], axis=-1, keepdims=True)) + lam_init)

    def load_queries(i):
        q0 = pl.multiple_of(i * tile, tile)
        qq = q_ref[pl.ds(q0, tile), :].astype(F32)
        qt_ref[i % 2, 0] = jnp.where(lo_half, qq, qfill1).T.astype(BF16)
        qt_ref[i % 2, 1] = jnp.where(lo_half, qfill2, qq).T.astype(BF16)

    def scores(i, j, dst, ms=(0, 1)):
        s_ref, mc_ref = dst
        k0 = pl.multiple_of(j * tile, tile)
        for m in ms:
            ka_ref = (k1_ref, k2_ref)[m]
            s = jnp.dot(ka_ref[pl.ds(k0, tile), :], qt_ref[i % 2, m], preferred_element_type=F32)
            s_ref[m] = s
            mc_ref[m] = jnp.max(s.reshape(grp, SUBLANES, tile), axis=0)

    half = tile // 2

    def scores_diag(i, dst, ms=(0, 1)):
        s_ref, _ = dst
        k0 = pl.multiple_of(i * tile, tile)
        k1 = pl.multiple_of(i * tile + half, half)
        for m in ms:
            ka_ref = (k1_ref, k2_ref)[m]
            s_ref[m, 0:half, :] = jnp.dot(ka_ref[pl.ds(k0, half), :], qt_ref[i % 2, m],
                                          preferred_element_type=F32)
            s_ref[m, half:tile, half:tile] = jnp.dot(ka_ref[pl.ds(k1, half), :],
                                                     qt_ref[i % 2, m, :, half:tile],
                                                     preferred_element_type=F32)

    def softmax_diag(i, src, ms=(0, 1)):
        s_ref, _ = src
        k0 = pl.multiple_of(i * tile, tile)
        k1 = pl.multiple_of(i * tile + half, half)
        hgrp = half // SUBLANES
        colmax = lambda x: jnp.max(x.reshape(hgrp, SUBLANES, half), axis=0)
        colsum = lambda x: jnp.sum(x.reshape(hgrp, SUBLANES, half), axis=0)
        dblk = dblk_ref[...]
        vt_lo = vt_ref[:, pl.ds(k0, half)]
        vt_hi = vt_ref[:, pl.ds(k1, half)]
        for m in ms:
            s_lo = s_ref[m, 0:half, 0:half] + dblk
            m_prev = m_ref[m, :, 0:half]
            m_new = jnp.maximum(m_prev, jnp.max(colmax(s_lo), axis=0, keepdims=True))
            alpha = jnp.exp2(m_prev - m_new)
            p_lo = jnp.exp2(s_lo - m_new)
            l_ref[m, :, 0:half] = alpha * l_ref[m, :, 0:half] + colsum(p_lo)
            a_ref[m, :, 0:half] = alpha * a_ref[m, :, 0:half] + jnp.dot(
                vt_lo, p_lo.astype(BF16), preferred_element_type=F32)
            m_ref[m, :, 0:half] = m_new
            s_ref[m, half - LANES:half, half:half + LANES] = (
                s_ref[m, half - LANES:half, half:half + LANES] + corner_ref[...])
            s_top = s_ref[m, 0:half, half:tile]
            s_bot = s_ref[m, half:tile, half:tile] + dblk
            m_prev = m_ref[m, :, half:tile]
            m_cur = jnp.maximum(colmax(s_top), colmax(s_bot))
            m_new = jnp.maximum(m_prev, jnp.max(m_cur, axis=0, keepdims=True))
            alpha = jnp.exp2(m_prev - m_new)
            p_top = jnp.exp2(s_top - m_new)
            p_bot = jnp.exp2(s_bot - m_new)
            l_ref[m, :, half:tile] = alpha * l_ref[m, :, half:tile] + colsum(p_top) + colsum(p_bot)
            a_ref[m, :, half:tile] = (alpha * a_ref[m, :, half:tile]
                                      + jnp.dot(vt_lo, p_top.astype(BF16), preferred_element_type=F32)
                                      + jnp.dot(vt_hi, p_bot.astype(BF16), preferred_element_type=F32))
            m_ref[m, :, half:tile] = m_new

    def near_fix(dst, ms=(0, 1)):
        s_ref, mc_ref = dst
        for m in ms:
            s_ref[m, tile - LANES:tile, 0:LANES] = (s_ref[m, tile - LANES:tile, 0:LANES]
                                                    + corner_ref[...])
            col = s_ref[m, :, 0:LANES]
            mc_ref[m, :, 0:LANES] = jnp.max(col.reshape(grp, SUBLANES, LANES), axis=0)

    def softmax_pv(j, src, ms=(0, 1)):
        s_ref, mc_ref = src
        k0 = pl.multiple_of(j * tile, tile)
        vt = vt_ref[:, pl.ds(k0, tile)]
        for m in ms:
            s = s_ref[m]
            m_prev = m_ref[m]
            m_new = jnp.maximum(m_prev, jnp.max(mc_ref[m], axis=0, keepdims=True))
            alpha = jnp.exp2(m_prev - m_new)
            p = jnp.exp2(s - m_new)
            l_ref[m] = alpha * l_ref[m] + jnp.sum(p.reshape(grp, SUBLANES, tile), axis=0)
            a_ref[m] = alpha * a_ref[m] + jnp.dot(vt, p.astype(BF16), preferred_element_type=F32)
            m_ref[m] = m_new

    def pipelined(i, j_next, dst, j_cur, src):
        for m in range(2):
            scores(i, j_next, dst, (m,))
            softmax_pv(j_cur, src, (m,))

    sa = (sa_ref, mca_ref)
    sb = (sb_ref, mcb_ref)
    sc = (sc_ref, mcc_ref)

    load_queries(0)
    scores_diag(0, sa)

    def q_body(i, carry):
        m_ref[...] = jnp.full(m_ref.shape, NEG, F32)
        l_ref[...] = jnp.zeros(l_ref.shape, F32)
        a_ref[...] = jnp.zeros(a_ref.shape, F32)
        even = (i % 2) == 0
        nxt = jnp.minimum(i + 1, nq - 1)

        @pl.when(even & (i >= 2))
        def _():
            pipelined(i, 1, sb, 0, sc)

        j_first = jnp.where(even, 1, 0)

        def far_pair(t, c2):
            j = j_first + 2 * t
            pipelined(i, j + 1, sa, j, sb)
            pipelined(i, j + 2, sb, j + 1, sa)
            return c2

        lax.fori_loop(0, jnp.maximum(i - 1, 0) // 2, far_pair, 0)

        @pl.when(i >= 1)
        def _():
            for m in range(2):
                scores_diag(i, sa, (m,))
                near_fix(sb, (m,))
                softmax_pv(i - 1, sb, (m,))

        @pl.when(even)
        def _():
            load_queries(nxt)
            for m in range(2):
                scores(nxt, 0, sb, (m,))
                softmax_diag(i, sa, (m,))

        @pl.when(jnp.logical_not(even))
        def _():
            load_queries(nxt)
            for m in range(2):
                scores(nxt, 0, sc, (m,))
                softmax_diag(i, sa, (m,))

        r1 = 1.0 / jnp.sum(l_ref[0], axis=0, keepdims=True)
        r2 = 1.0 / jnp.sum(l_ref[1], axis=0, keepdims=True)
        ot = a_ref[0] * r1 - a_ref[1] * (lam * r2)
        ms = jnp.mean(ot * ot, axis=0, keepdims=True)
        y = (ot * lax.rsqrt(ms + EPS)).T * g_ref[...] * (1.0 - lam_init)
        q0 = pl.multiple_of(i * tile, tile)
        o_ref[pl.ds(q0, tile), :] = y.astype(o_ref.dtype)
        return carry

    lax.fori_loop(0, nq, q_body, 0)


def _t5_bucket(rel):
    nb = REL_BUCKETS // 2
    max_exact = nb // 2
    ret = (rel > 0).astype(jnp.int32) * nb
    n = jnp.abs(rel)
    nf = jnp.maximum(n, 1).astype(jnp.float32)
    large = max_exact + (jnp.log(nf / max_exact) / math.log(REL_MAX_DIST / max_exact)
                         * (nb - max_exact)).astype(jnp.int32)
    large = jnp.minimum(large, nb - 1)
    return ret + jnp.where(n < max_exact, n, large)


def _bias_tiles(rel_table, block):
    table = rel_table.astype(F32) * LOG2E
    cfar = table[_t5_bucket(jnp.int32(-REL_MAX_DIST))]

    def toeplitz(first_rel, n):
        rel = first_rel + jnp.arange(2 * n - 1, dtype=jnp.int32)
        vec = (table[_t5_bucket(rel)] - cfar[None, :]).T
        heads = vec.shape[0]
        vp = jnp.pad(vec, ((0, 0), (0, 1)))
        skew = jnp.tile(vp, (1, n))[:, :n * (2 * n - 1)].reshape(heads, n, 2 * n - 1)
        return skew[:, :, n - 1:].transpose(0, 2, 1)

    pos = jnp.arange(block, dtype=jnp.int32)
    allowed = (pos[:, None] // CHUNK) <= (pos[None, :] // CHUNK)
    diag = jnp.where(allowed[None], toeplitz(-(block - 1), block), NEG)
    corner = toeplitz(-(2 * LANES - 1), LANES)
    return diag, corner, cfar


def _attention(qkv, B, S, lam_params, subln_g, bias_diag, bias_corner, cfar, layer_idx):
    T = B * S
    tile = ATT_TILE
    assert tile // 2 >= LANES >= REL_MAX_DIST and S % tile == 0
    lam_init = 0.8 - 0.6 * math.exp(-0.3 * layer_idx)
    kern = functools.partial(_attn_kernel, seq=S, tile=tile, lam_init=lam_init)
    H = DA_HEADS
    return pl.pallas_call(
        kern,
        grid=(H, B),
        in_specs=[
            pl.BlockSpec(memory_space=pltpu.SMEM),
            pl.BlockSpec((4, DA_HEAD_DIM), lambda h, b: (0, 0)),
            pl.BlockSpec((S, LANES), lambda h, b: (b, h)),
            pl.BlockSpec((S, LANES), lambda h, b: (b, H + h)),
            pl.BlockSpec((S, LANES), lambda h, b: (b, 2 * H + h)),
            pl.BlockSpec((None, tile // 2, tile // 2), lambda h, b: (h, 0, 0)),
            pl.BlockSpec((None, LANES, LANES), lambda h, b: (h, 0, 0)),
            pl.BlockSpec((1, DA_V_DIM), lambda h, b: (0, 0)),
        ],
        out_specs=pl.BlockSpec((S, LANES), lambda h, b: (b, h)),
        out_shape=jax.ShapeDtypeStruct((T, H * DA_V_DIM), BF16),
        scratch_shapes=[
            pltpu.VMEM((S, LANES), BF16), pltpu.VMEM((S, LANES), BF16), pltpu.VMEM((DA_V_DIM, S), BF16),
            pltpu.VMEM((2, 2, LANES, tile), BF16),
            pltpu.VMEM((2, tile, tile), F32), pltpu.VMEM((2, tile, tile), F32),
            pltpu.VMEM((2, tile, tile), F32),
            pltpu.VMEM((2, SUBLANES, tile), F32), pltpu.VMEM((2, SUBLANES, tile), F32),
            pltpu.VMEM((2, SUBLANES, tile), F32),
            pltpu.VMEM((2, 1, tile), F32), pltpu.VMEM((2, SUBLANES, tile), F32),
            pltpu.VMEM((2, DA_V_DIM, tile), F32),
        ],
        compiler_params=_params("parallel", "parallel"),
        name="diff_attn",
    )(cfar, lam_params.astype(F32), qkv, qkv, qkv, bias_diag, bias_corner,
      subln_g.reshape(1, DA_V_DIM).astype(F32))


def _merge_kernel(ya_ref, yb_ref, yc_ref, g0_ref, g1_ref, g2_ref, bg_ref, wb_ref, wo_ref, x_ref, o_ref):
    merged = None
    for k, (y_ref, g_ref) in enumerate(((ya_ref, g0_ref), (yb_ref, g1_ref), (yc_ref, g2_ref))):
        gate = jax.nn.sigmoid(g_ref[...].astype(F32) + bg_ref[k:k + 1, :])
        term = gate * jnp.dot(y_ref[...], wb_ref[k], preferred_element_type=F32)
        merged = term if merged is None else merged + term
    o_ref[...] = x_ref[...] + jnp.dot(merged.astype(BF16), wo_ref[...], preferred_element_type=F32)


def _merge(ya, yb, yc, gates, b_gate, w_branch, w_out, x2):
    T, D = x2.shape
    tm = min(512, T)
    tok = lambda i: (i, 0)
    return pl.pallas_call(
        _merge_kernel,
        grid=(T // tm,),
        in_specs=[
            pl.BlockSpec((tm, D), tok), pl.BlockSpec((tm, D), tok), pl.BlockSpec((tm, D), tok),
            pl.BlockSpec((tm, D), lambda i: (i, 0)),
            pl.BlockSpec((tm, D), lambda i: (i, 1)),
            pl.BlockSpec((tm, D), lambda i: (i, 2)),
            pl.BlockSpec((N_BRANCH, D), lambda i: (0, 0)),
            pl.BlockSpec((N_BRANCH, D, D), lambda i: (0, 0, 0)),
            pl.BlockSpec((D, D), lambda i: (0, 0)),
            pl.BlockSpec((tm, D), tok),
        ],
        out_specs=pl.BlockSpec((tm, D), tok),
        out_shape=jax.ShapeDtypeStruct((T, D), F32),
        compiler_params=_params("parallel"),
        name="merge",
    )(ya, yb, yc, gates, gates, gates, b_gate.astype(F32), w_branch, w_out, x2)


def _mlp_kernel(x_ref, g_ref, wu_ref, wd_ref, gf_ref, o_ref, *, ft, final_norm):
    x = x_ref[...]
    h = _rms(x, g_ref[...]).astype(BF16)
    acc = x
    for f in range(wu_ref.shape[1] // ft):
        hid = jnp.dot(h, wu_ref[:, f * ft:(f + 1) * ft], preferred_element_type=F32)
        hid = jnp.square(jnp.maximum(hid, 0.0)).astype(BF16)
        acc = acc + jnp.dot(hid, wd_ref[f * ft:(f + 1) * ft, :], preferred_element_type=F32)
    if final_norm:
        acc = _rms(acc, gf_ref[...])
    o_ref[...] = acc


def _mlp(x2, g, w_up, w_down, g_final, final_norm):
    T, D = x2.shape
    F = w_up.shape[1]
    tm = min(512, T)
    kern = functools.partial(_mlp_kernel, ft=1024, final_norm=final_norm)
    return pl.pallas_call(
        kern,
        grid=(T // tm,),
        in_specs=[
            pl.BlockSpec((tm, D), lambda i: (i, 0)),
            pl.BlockSpec((1, D), lambda i: (0, 0)),
            pl.BlockSpec((D, F), lambda i: (0, 0)),
            pl.BlockSpec((F, D), lambda i: (0, 0)),
            pl.BlockSpec((1, D), lambda i: (0, 0)),
        ],
        out_specs=pl.BlockSpec((tm, D), lambda i: (i, 0)),
        out_shape=jax.ShapeDtypeStruct((T, D), F32),
        compiler_params=_params("parallel"),
        name="mlp",
    )(x2, g.reshape(1, D).astype(F32), w_up, w_down, g_final.reshape(1, D).astype(F32))


def kernel(x, w_in, g_mix, da_lambda, da_subln_g, rel_bias, sg_ln_g, sg_ln_b, sg_w, sg_b, lru_conv_w, lru_conv_b, lru_w_a, lru_b_a, lru_w_i, lru_b_i, lru_lambda, b_gate, w_branch, w_out, g_mlp, w_up, w_down, g_final):
    B, S, D = x.shape
    T = B * S
    depth = w_in.shape[0]
    x2 = x.reshape(T, D).astype(F32)

    qscale = jnp.concatenate([jnp.full((D,), DA_HEAD_DIM ** -0.5 * LOG2E, F32),
                              jnp.ones((w_in.shape[2] - D,), F32)])
    bias_diag, bias_corner, cfar = _bias_tiles(rel_bias, ATT_TILE // 2)

    for l in range(depth):
        w_l = (w_in[l] * qscale[None, :]).astype(BF16)
        qkv, gates, yb, yc = _front(x2, S, g_mix[l], w_l, sg_ln_g[l], sg_ln_b[l], sg_w[l], sg_b[l],
                                    lru_conv_w[l], lru_conv_b[l], lru_w_a[l], lru_b_a[l],
                                    lru_w_i[l], lru_b_i[l], lru_lambda[l])
        ya = _attention(qkv, B, S, da_lambda[l], da_subln_g[l], bias_diag, bias_corner, cfar, l)
        x2 = _merge(ya, yb, yc, gates, b_gate[l], w_branch[l].astype(BF16),
                    w_out[l].astype(BF16), x2)
        x2 = _mlp(x2, g_mlp[l], w_up[l].astype(BF16), w_down[l].astype(BF16), g_final,
                  final_norm=(l == depth - 1))
    return x2.reshape(B, S, D).astype(x.dtype)
```

```python
import functools
import math

import jax
import jax.numpy as jnp
from jax import lax
from jax.experimental import pallas as pl
from jax.experimental.pallas import tpu as pltpu

EPS = 1e-6
CHUNK = 64
DA_HEADS = 8
DA_HEAD_DIM = 64
DA_V_DIM = 2 * DA_HEAD_DIM
REL_BUCKETS = 32
REL_MAX_DIST = 128
SG_BLOCK = 128
SG_GROUPS = 4
LRU_BLOCKS = 8
CONV_WIDTH = 4
LRU_C = 8.0
N_BRANCH = 3

LANES = 128
SUBLANES = 8
VMEM_LIMIT = 56 * 1024 * 1024
NEG = -1e30
LOG2E = math.log2(math.e)

FRONT_ROWS = 512
ATT_TILE = 512
BF16 = jnp.bfloat16
F32 = jnp.float32


def _params(*sem):
    return pltpu.CompilerParams(dimension_semantics=sem, vmem_limit_bytes=VMEM_LIMIT)


def _rms(x, g):
    ms = jnp.mean(x * x, axis=-1, keepdims=True)
    return x * lax.rsqrt(ms + EPS) * g


COL_Q, COL_U, COL_V, COL_XR, COL_XG, COL_GATE = 0, 3, 4, 5, 6, 7
MXU_COLS = 256
LRU_UNIT_ROWS = 128


def _front_kernel(x_ref, g_ref, w_ref, lng_ref, lnb_ref, ws_ref, bs_ref,
                  cw_ref, cb_ref, wa_ref, ba_ref, wi_ref, bi_ref, lam_ref,
                  qkv_ref, gates_ref, yb_ref, yc_ref,
                  h_ref, ext_ref, a_ref, b_ref, v_ref, hcar_ref, *, rows, tiles_per_seq):
    t = pl.program_id(0) % tiles_per_seq
    D = x_ref.shape[1]
    halo = SUBLANES
    nt = MXU_COLS
    n_tiles = D // nt
    ru = LRU_UNIT_ROWS
    h_ref[...] = _rms(x_ref[...], g_ref[...]).astype(BF16)

    def piece(block, n):
        c0 = block * D + n * nt
        return jnp.dot(h_ref[...], w_ref[:, c0:c0 + nt], preferred_element_type=F32)

    def store_piece(dst_ref, dst_block, block, n):
        dst_ref[:, dst_block * D + n * nt:dst_block * D + (n + 1) * nt] = (
            piece(block, n).astype(dst_ref.dtype))

    def xr_piece(n):
        ext_ref[halo:halo + rows, n * nt:(n + 1) * nt] = piece(COL_XR, n)

    def gate_piece(n):
        yc_ref[:, n * nt:(n + 1) * nt] = jax.nn.gelu(piece(COL_XG, n)).astype(yc_ref.dtype)

    def v_piece(n):
        v_ref[:, n * nt:(n + 1) * nt] = piece(COL_V, n)

    pieces = ([functools.partial(xr_piece, n) for n in range(1, n_tiles)]
              + [functools.partial(gate_piece, n) for n in range(n_tiles)]
              + [functools.partial(v_piece, n) for n in range(n_tiles)]
              + [functools.partial(store_piece, qkv_ref, c, COL_Q + c, n)
                 for c in range(3) for n in range(n_tiles)]
              + [functools.partial(store_piece, gates_ref, c, COL_GATE + c, n)
                 for c in range(N_BRANCH) for n in range(n_tiles)])

    def spend(k=1):
        for _ in range(k):
            if pieces:
                pieces.pop(0)()

    @pl.when(t == 0)
    def _():
        hcar_ref[...] = jnp.zeros(hcar_ref.shape, F32)
        ext_ref[0:halo, :] = jnp.zeros((halo, D), F32)

    @pl.when(t != 0)
    def _():
        ext_ref[0:halo, :] = ext_ref[rows:rows + halo, :]

    bd = D // LRU_BLOCKS
    row = lax.broadcasted_iota(jnp.int32, (1, SUBLANES, nt), 1)

    def lru_unit(n, rb):
        cs = slice(n * nt, (n + 1) * nt)
        r0 = rb * ru
        xc = jnp.zeros((ru, nt), F32) + cb_ref[:, cs]
        for j in range(CONV_WIDTH):
            off = halo - (CONV_WIDTH - 1) + j + r0
            xc = xc + cw_ref[j:j + 1, cs] * ext_ref[off:off + ru, cs]
        spend()
        nz = -lam_ref[:, cs]
        softplus_neg = jnp.maximum(nz, 0.0) + jnp.log1p(jnp.exp(-jnp.abs(nz)))
        xcb = xc.astype(BF16)
        r_parts, i_parts = [], []
        for hb in range(n * nt // bd, (n + 1) * nt // bd):
            xs = xcb[:, hb * bd - n * nt:(hb + 1) * bd - n * nt]
            r_parts.append(jnp.dot(xs, wa_ref[hb], preferred_element_type=F32))
            i_parts.append(jnp.dot(xs, wi_ref[hb], preferred_element_type=F32))
        r = jax.nn.sigmoid(jnp.concatenate(r_parts, axis=1) + ba_ref[:, cs])
        ig = jax.nn.sigmoid(jnp.concatenate(i_parts, axis=1) + bi_ref[:, cs])
        log_a = -LRU_C * r * softplus_neg
        a = jnp.exp(log_a)
        y = jnp.maximum(-jnp.tanh(log_a) * (a * a + 1.0), 0.0)
        u = xc * ig * jnp.where(y > 0.0, y * lax.rsqrt(y), 0.0)
        a3 = a.reshape(ru // SUBLANES, SUBLANES, nt)
        b3 = u.reshape(ru // SUBLANES, SUBLANES, nt)
        s = 1
        while s < SUBLANES:
            keep = row >= s
            a_sh = pltpu.roll(a3, s, axis=1)
            b_sh = pltpu.roll(b3, s, axis=1)
            b3 = jnp.where(keep, b3 + a3 * b_sh, b3)
            a3 = jnp.where(keep, a3 * a_sh, a3)
            s *= 2
        a_ref[r0:r0 + ru, cs] = a3.reshape(ru, nt)
        b_ref[r0:r0 + ru, cs] = b3.reshape(ru, nt)

    xr_piece(0)
    for n in range(n_tiles):
        for rb in range(rows // ru):
            spend()
            lru_unit(n, rb)

    def sgu_norm(rb):
        rs = slice(rb * ru, (rb + 1) * ru)
        v = jax.nn.gelu(v_ref[rs, :])
        mu = jnp.mean(v, axis=-1, keepdims=True)
        vc = v - mu
        var = jnp.mean(vc * vc, axis=-1, keepdims=True)
        v_ref[rs, :] = vc * lax.rsqrt(var + EPS) * lng_ref[...] + lnb_ref[...]

    hprev = jnp.broadcast_to(hcar_ref[...], (SUBLANES, D))
    grp_per_unit = ru // SUBLANES
    for i in range(rows // SUBLANES):
        r0 = i * SUBLANES
        hh = b_ref[r0:r0 + SUBLANES, :] + a_ref[r0:r0 + SUBLANES, :] * hprev
        b_ref[r0:r0 + SUBLANES, :] = hh
        hprev = jnp.broadcast_to(hh[SUBLANES - 1:SUBLANES, :], (SUBLANES, D))
        if (i + 1) % grp_per_unit == 0:
            rs = slice(r0 + SUBLANES - ru, r0 + SUBLANES)
            yc_ref[rs, :] = (b_ref[rs, :] * yc_ref[rs, :].astype(F32)).astype(yc_ref.dtype)
            sgu_norm(i // grp_per_unit)
            spend()
    hcar_ref[...] = hprev[0:1, :]

    vn = v_ref[...].astype(BF16)
    pos_q = lax.broadcasted_iota(jnp.int32, (SG_BLOCK, SG_BLOCK), 0)
    pos_k = lax.broadcasted_iota(jnp.int32, (SG_BLOCK, SG_BLOCK), 1)
    mask = (pos_k // CHUNK) <= (pos_q // CHUNK)
    gd = D // SG_GROUPS
    for g in range(SG_GROUPS):
        spend()
        w = jnp.where(mask, ws_ref[g], 0.0).astype(BF16)
        cs = slice(g * gd, (g + 1) * gd)
        ug = jax.nn.gelu(jnp.dot(h_ref[...], w_ref[:, COL_U * D + g * gd:COL_U * D + (g + 1) * gd],
                                 preferred_element_type=F32))
        for nb in range(rows // SG_BLOCK):
            rs = slice(nb * SG_BLOCK, (nb + 1) * SG_BLOCK)
            mixed = jnp.dot(w, vn[rs, cs], preferred_element_type=F32) + bs_ref[:, cs]
            yb_ref[rs, cs] = (ug[rs, :] * mixed).astype(yb_ref.dtype)
    spend(len(pieces))


def _front(x2, S, g, w, ln_g, ln_b, w_s, b_s, conv_w, conv_b, w_a, b_a, w_i, b_i, lam):
    T, D = x2.shape
    N = w.shape[1]
    tm = min(FRONT_ROWS, S)
    gd = D // SG_GROUPS
    bsb = jnp.repeat(b_s.T.astype(F32), gd, axis=1)
    vec = lambda a: a.reshape(1, D).astype(F32)
    kern = functools.partial(_front_kernel, rows=tm, tiles_per_seq=S // tm)
    const2 = lambda i: (0, 0)
    const3 = lambda i: (0, 0, 0)
    once = pl.Buffered(1)
    nb = D // LRU_BLOCKS
    tok = lambda i: (i, 0)
    return pl.pallas_call(
        kern,
        grid=(T // tm,),
        in_specs=[
            pl.BlockSpec((tm, D), tok),
            pl.BlockSpec((1, D), const2),
            pl.BlockSpec((D, N), const2, pipeline_mode=once),
            pl.BlockSpec((1, D), const2), pl.BlockSpec((1, D), const2),
            pl.BlockSpec((SG_GROUPS, SG_BLOCK, SG_BLOCK), const3),
            pl.BlockSpec((SG_BLOCK, D), const2),
            pl.BlockSpec((CONV_WIDTH, D), const2), pl.BlockSpec((1, D), const2),
            pl.BlockSpec((LRU_BLOCKS, nb, nb), const3), pl.BlockSpec((1, D), const2),
            pl.BlockSpec((LRU_BLOCKS, nb, nb), const3), pl.BlockSpec((1, D), const2),
            pl.BlockSpec((1, D), const2),
        ],
        out_specs=[pl.BlockSpec((tm, 3 * D), tok), pl.BlockSpec((tm, 3 * D), tok),
                   pl.BlockSpec((tm, D), tok), pl.BlockSpec((tm, D), tok)],
        out_shape=[jax.ShapeDtypeStruct((T, 3 * D), BF16), jax.ShapeDtypeStruct((T, 3 * D), BF16),
                   jax.ShapeDtypeStruct((T, D), BF16), jax.ShapeDtypeStruct((T, D), BF16)],
        scratch_shapes=[
            pltpu.VMEM((tm, D), BF16),
            pltpu.VMEM((tm + SUBLANES, D), F32),
            pltpu.VMEM((tm, D), F32), pltpu.VMEM((tm, D), F32), pltpu.VMEM((tm, D), F32),
            pltpu.VMEM((1, D), F32),
        ],
        compiler_params=_params("arbitrary"),
        name="front",
    )(x2, vec(g), w, vec(ln_g), vec(ln_b), w_s.astype(F32), bsb,
      conv_w.astype(F32), vec(conv_b), w_a.astype(BF16), vec(b_a), w_i.astype(BF16), vec(b_i),
      vec(lam))


def _attn_kernel(cfar_ref, lp_ref, q_ref, k_ref, v_ref, dblk_ref, corner_ref, g_ref, o_ref,
                 k1_ref, k2_ref, vt_ref, qt_ref, sa_ref, sb_ref, sc_ref, mca_ref, mcb_ref, mcc_ref,
                 m_ref, l_ref, a_ref,
                 *, seq, tile, lam_init):
    h = pl.program_id(0)
    nq = seq // tile
    grp = tile // SUBLANES
    lane = lax.broadcasted_iota(jnp.int32, (1, LANES), 1)
    lo_half = lane < DA_HEAD_DIM

    c = jnp.full((1, LANES), cfar_ref[h], F32)
    c_hi = c.astype(BF16).astype(F32)
    c_lo = c - c_hi
    kfill1 = jnp.where(lane == DA_HEAD_DIM, c_hi, jnp.where(lane == DA_HEAD_DIM + 1, c_lo, 0.0))
    kfill2 = jnp.where(lane == 0, c_hi, jnp.where(lane == 1, c_lo, 0.0))
    qfill1 = jnp.where((lane == DA_HEAD_DIM) | (lane == DA_HEAD_DIM + 1), 1.0, 0.0)
    qfill2 = jnp.where((lane == 0) | (lane == 1), 1.0, 0.0)

    kk = k_ref[...].astype(F32)
    k1_ref[...] = jnp.where(lo_half, kk, kfill1).astype(BF16)
    k2_ref[...] = jnp.where(lo_half, kfill2, kk).astype(BF16)
    for c0 in range(0, seq, tile):
        vt_ref[:, c0:c0 + tile] = v_ref[c0:c0 + tile, :].astype(F32).T.astype(BF16)

    lp = lp_ref[...]
    lam = (jnp.exp(jnp.sum(lp[0:1] * lp[1:2], axis=-1, keepdims=True))
           - jnp.exp(jnp.sum(lp[2:3] * lp[3:4], axis=-1, keepdims=True)) + lam_init)

    def load_queries(i):
        q0 = pl.multiple_of(i * tile, tile)
        qq = q_ref[pl.ds(q0, tile), :].astype(F32)
        qt_ref[i % 2, 0] = jnp.where(lo_half, qq, qfill1).T.astype(BF16)
        qt_ref[i % 2, 1] = jnp.where(lo_half, qfill2, qq).T.astype(BF16)

    def scores(i, j, dst, ms=(0, 1)):
        s_ref, mc_ref = dst
        k0 = pl.multiple_of(j * tile, tile)
        for m in ms:
            ka_ref = (k1_ref, k2_ref)[m]
            s = jnp.dot(ka_ref[pl.ds(k0, tile), :], qt_ref[i % 2, m], preferred_element_type=F32)
            s_ref[m] = s
            mc_ref[m] = jnp.max(s.reshape(grp, SUBLANES, tile), axis=0)

    half = tile // 2

    def scores_diag(i, dst, ms=(0, 1)):
        s_ref, _ = dst
        k0 = pl.multiple_of(i * tile, tile)
        k1 = pl.multiple_of(i * tile + half, half)
        for m in ms:
            ka_ref = (k1_ref, k2_ref)[m]
            s_ref[m, 0:half, :] = jnp.dot(ka_ref[pl.ds(k0, half), :], qt_ref[i % 2, m],
                                          preferred_element_type=F32)
            s_ref[m, half:tile, half:tile] = jnp.dot(ka_ref[pl.ds(k1, half), :],
                                                     qt_ref[i % 2, m, :, half:tile],
                                                     preferred_element_type=F32)

    def softmax_diag(i, src, ms=(0, 1)):
        s_ref, _ = src
        k0 = pl.multiple_of(i * tile, tile)
        k1 = pl.multiple_of(i * tile + half, half)
        hgrp = half // SUBLANES
        colmax = lambda x: jnp.max(x.reshape(hgrp, SUBLANES, half), axis=0)
        colsum = lambda x: jnp.sum(x.reshape(hgrp, SUBLANES, half), axis=0)
        dblk = dblk_ref[...]
        vt_lo = vt_ref[:, pl.ds(k0, half)]
        vt_hi = vt_ref[:, pl.ds(k1, half)]
        for m in ms:
            s_lo = s_ref[m, 0:half, 0:half] + dblk
            m_prev = m_ref[m, :, 0:half]
            m_new = jnp.maximum(m_prev, jnp.max(colmax(s_lo), axis=0, keepdims=True))
            alpha = jnp.exp2(m_prev - m_new)
            p_lo = jnp.exp2(s_lo - m_new)
            l_ref[m, :, 0:half] = alpha * l_ref[m, :, 0:half] + colsum(p_lo)
            a_ref[m, :, 0:half] = alpha * a_ref[m, :, 0:half] + jnp.dot(
                vt_lo, p_lo.astype(BF16), preferred_element_type=F32)
            m_ref[m, :, 0:half] = m_new
            s_ref[m, half - LANES:half, half:half + LANES] = (
                s_ref[m, half - LANES:half, half:half + LANES] + corner_ref[...])
            s_top = s_ref[m, 0:half, half:tile]
            s_bot = s_ref[m, half:tile, half:tile] + dblk
            m_prev = m_ref[m, :, half:tile]
            m_cur = jnp.maximum(colmax(s_top), colmax(s_bot))
            m_new = jnp.maximum(m_prev, jnp.max(m_cur, axis=0, keepdims=True))
            alpha = jnp.exp2(m_prev - m_new)
            p_top = jnp.exp2(s_top - m_new)
            p_bot = jnp.exp2(s_bot - m_new)
            l_ref[m, :, half:tile] = alpha * l_ref[m, :, half:tile] + colsum(p_top) + colsum(p_bot)
            a_ref[m, :, half:tile] = (alpha * a_ref[m, :, half:tile]
                                      + jnp.dot(vt_lo, p_top.astype(BF16), preferred_element_type=F32)
                                      + jnp.dot(vt_hi, p_bot.astype(BF16), preferred_element_type=F32))
            m_ref[m, :, half:tile] = m_new

    def near_fix(dst, ms=(0, 1)):
        s_ref, mc_ref = dst
        for m in ms:
            s_ref[m, tile - LANES:tile, 0:LANES] = (s_ref[m, tile - LANES:tile, 0:LANES]
                                                    + corner_ref[...])
            col = s_ref[m, :, 0:LANES]
            mc_ref[m, :, 0:LANES] = jnp.max(col.reshape(grp, SUBLANES, LANES), axis=0)

    def softmax_pv(j, src, ms=(0, 1)):
        s_ref, mc_ref = src
        k0 = pl.multiple_of(j * tile, tile)
        vt = vt_ref[:, pl.ds(k0, tile)]
        for m in ms:
            s = s_ref[m]
            m_prev = m_ref[m]
            m_new = jnp.maximum(m_prev, jnp.max(mc_ref[m], axis=0, keepdims=True))
            alpha = jnp.exp2(m_prev - m_new)
            p = jnp.exp2(s - m_new)
            l_ref[m] = alpha * l_ref[m] + jnp.sum(p.reshape(grp, SUBLANES, tile), axis=0)
            a_ref[m] = alpha * a_ref[m] + jnp.dot(vt, p.astype(BF16), preferred_element_type=F32)
            m_ref[m] = m_new

    def pipelined(i, j_next, dst, j_cur, src):
        for m in range(2):
            scores(i, j_next, dst, (m,))
            softmax_pv(j_cur, src, (m,))

    sa = (sa_ref, mca_ref)
    sb = (sb_ref, mcb_ref)
    sc = (sc_ref, mcc_ref)

    load_queries(0)
    scores_diag(0, sa)

    def reset_stats():
        m_ref[...] = jnp.full(m_ref.shape, NEG, F32)
        l_ref[...] = jnp.zeros(l_ref.shape, F32)
        a_ref[...] = jnp.zeros(a_ref.shape, F32)

    def finish(i):
        r1 = 1.0 / jnp.sum(l_ref[0], axis=0, keepdims=True)
        r2 = 1.0 / jnp.sum(l_ref[1], axis=0, keepdims=True)
        ot = a_ref[0] * r1 - a_ref[1] * (lam * r2)
        ms = jnp.mean(ot * ot, axis=0, keepdims=True)
        y = (ot * lax.rsqrt(ms + EPS)).T * g_ref[...] * (1.0 - lam_init)
        q0 = pl.multiple_of(i * tile, tile)
        o_ref[pl.ds(q0, tile), :] = y.astype(o_ref.dtype)
        reset_stats()

    reset_stats()

    def q_body(i, carry):
        even = (i % 2) == 0
        nxt = jnp.minimum(i + 1, nq - 1)

        @pl.when(even & (i >= 2))
        def _():
            pipelined(i, 1, sb, 0, sc)

        j_first = jnp.where(even, 1, 0)

        def far_pair(t, c2):
            j = j_first + 2 * t
            pipelined(i, j + 1, sa, j, sb)
            pipelined(i, j + 2, sb, j + 1, sa)
            return c2

        lax.fori_loop(0, jnp.maximum(i - 1, 0) // 2, far_pair, 0)

        @pl.when(i >= 1)
        def _():
            for m in range(2):
                scores_diag(i, sa, (m,))
                near_fix(sb, (m,))
                softmax_pv(i - 1, sb, (m,))

        def diag_step(dst):
            load_queries(nxt)
            scores(nxt, 0, dst, (0,))
            softmax_diag(i, sa, (0,))
            softmax_diag(i, sa, (1,))
            scores(nxt, 0, dst, (1,))
            finish(i)

        @pl.when(even)
        def _():
            diag_step(sb)

        @pl.when(jnp.logical_not(even))
        def _():
            diag_step(sc)

        return carry

    lax.fori_loop(0, nq, q_body, 0)


def _t5_bucket(rel):
    nb = REL_BUCKETS // 2
    max_exact = nb // 2
    ret = (rel > 0).astype(jnp.int32) * nb
    n = jnp.abs(rel)
    nf = jnp.maximum(n, 1).astype(jnp.float32)
    large = max_exact + (jnp.log(nf / max_exact) / math.log(REL_MAX_DIST / max_exact)
                         * (nb - max_exact)).astype(jnp.int32)
    large = jnp.minimum(large, nb - 1)
    return ret + jnp.where(n < max_exact, n, large)


def _bias_tiles(rel_table, block):
    table = rel_table.astype(F32) * LOG2E
    cfar = table[_t5_bucket(jnp.int32(-REL_MAX_DIST))]

    def toeplitz(first_rel, n):
        rel = first_rel + jnp.arange(2 * n - 1, dtype=jnp.int32)
        vec = (table[_t5_bucket(rel)] - cfar[None, :]).T
        heads = vec.shape[0]
        vp = jnp.pad(vec, ((0, 0), (0, 1)))
        skew = jnp.tile(vp, (1, n))[:, :n * (2 * n - 1)].reshape(heads, n, 2 * n - 1)
        return skew[:, :, n - 1:].transpose(0, 2, 1)

    pos = jnp.arange(block, dtype=jnp.int32)
    allowed = (pos[:, None] // CHUNK) <= (pos[None, :] // CHUNK)
    diag = jnp.where(allowed[None], toeplitz(-(block - 1), block), NEG)
    corner = toeplitz(-(2 * LANES - 1), LANES)
    return diag, corner, cfar


def _attention(qkv, B, S, lam_params, subln_g, bias_diag, bias_corner, cfar, layer_idx):
    T = B * S
    tile = ATT_TILE
    assert tile // 2 >= LANES >= REL_MAX_DIST and S % tile == 0
    lam_init = 0.8 - 0.6 * math.exp(-0.3 * layer_idx)
    kern = functools.partial(_attn_kernel, seq=S, tile=tile, lam_init=lam_init)
    H = DA_HEADS
    return pl.pallas_call(
        kern,
        grid=(H, B),
        in_specs=[
            pl.BlockSpec(memory_space=pltpu.SMEM),
            pl.BlockSpec((4, DA_HEAD_DIM), lambda h, b: (0, 0)),
            pl.BlockSpec((S, LANES), lambda h, b: (b, h)),
            pl.BlockSpec((S, LANES), lambda h, b: (b, H + h)),
            pl.BlockSpec((S, LANES), lambda h, b: (b, 2 * H + h)),
            pl.BlockSpec((None, tile // 2, tile // 2), lambda h, b: (h, 0, 0)),
            pl.BlockSpec((None, LANES, LANES), lambda h, b: (h, 0, 0)),
            pl.BlockSpec((1, DA_V_DIM), lambda h, b: (0, 0)),
        ],
        out_specs=pl.BlockSpec((S, LANES), lambda h, b: (b, h)),
        out_shape=jax.ShapeDtypeStruct((T, H * DA_V_DIM), BF16),
        scratch_shapes=[
            pltpu.VMEM((S, LANES), BF16), pltpu.VMEM((S, LANES), BF16), pltpu.VMEM((DA_V_DIM, S), BF16),
            pltpu.VMEM((2, 2, LANES, tile), BF16),
            pltpu.VMEM((2, tile, tile), F32), pltpu.VMEM((2, tile, tile), F32),
            pltpu.VMEM((2, tile, tile), F32),
            pltpu.VMEM((2, SUBLANES, tile), F32), pltpu.VMEM((2, SUBLANES, tile), F32),
            pltpu.VMEM((2, SUBLANES, tile), F32),
            pltpu.VMEM((2, 1, tile), F32), pltpu.VMEM((2, SUBLANES, tile), F32),
            pltpu.VMEM((2, DA_V_DIM, tile), F32),
        ],
        compiler_params=_params("parallel", "parallel"),
        name="diff_attn",
    )(cfar, lam_params.astype(F32), qkv, qkv, qkv, bias_diag, bias_corner,
      subln_g.reshape(1, DA_V_DIM).astype(F32))


def _merge_kernel(ya_ref, yb_ref, yc_ref, g0_ref, g1_ref, g2_ref, bg_ref, wb_ref, wo_ref, x_ref, o_ref):
    merged = None
    for k, (y_ref, g_ref) in enumerate(((ya_ref, g0_ref), (yb_ref, g1_ref), (yc_ref, g2_ref))):
        gate = jax.nn.sigmoid(g_ref[...].astype(F32) + bg_ref[k:k + 1, :])
        term = gate * jnp.dot(y_ref[...], wb_ref[k], preferred_element_type=F32)
        merged = term if merged is None else merged + term
    o_ref[...] = x_ref[...] + jnp.dot(merged.astype(BF16), wo_ref[...], preferred_element_type=F32)


def _merge(ya, yb, yc, gates, b_gate, w_branch, w_out, x2):
    T, D = x2.shape
    tm = min(512, T)
    tok = lambda i: (i, 0)
    return pl.pallas_call(
        _merge_kernel,
        grid=(T // tm,),
        in_specs=[
            pl.BlockSpec((tm, D), tok), pl.BlockSpec((tm, D), tok), pl.BlockSpec((tm, D), tok),
            pl.BlockSpec((tm, D), lambda i: (i, 0)),
            pl.BlockSpec((tm, D), lambda i: (i, 1)),
            pl.BlockSpec((tm, D), lambda i: (i, 2)),
            pl.BlockSpec((N_BRANCH, D), lambda i: (0, 0)),
            pl.BlockSpec((N_BRANCH, D, D), lambda i: (0, 0, 0)),
            pl.BlockSpec((D, D), lambda i: (0, 0)),
            pl.BlockSpec((tm, D), tok),
        ],
        out_specs=pl.BlockSpec((tm, D), tok),
        out_shape=jax.ShapeDtypeStruct((T, D), F32),
        compiler_params=_params("parallel"),
        name="merge",
    )(ya, yb, yc, gates, gates, gates, b_gate.astype(F32), w_branch, w_out, x2)


def _mlp_kernel(x_ref, g_ref, wu_ref, wd_ref, gf_ref, o_ref, *, ft, final_norm):
    x = x_ref[...]
    h = _rms(x, g_ref[...]).astype(BF16)
    acc = x
    for f in range(wu_ref.shape[1] // ft):
        hid = jnp.dot(h, wu_ref[:, f * ft:(f + 1) * ft], preferred_element_type=F32)
        hid = jnp.square(jnp.maximum(hid, 0.0)).astype(BF16)
        acc = acc + jnp.dot(hid, wd_ref[f * ft:(f + 1) * ft, :], preferred_element_type=F32)
    if final_norm:
        acc = _rms(acc, gf_ref[...])
    o_ref[...] = acc


def _mlp(x2, g, w_up, w_down, g_final, final_norm):
    T, D = x2.shape
    F = w_up.shape[1]
    tm = min(512, T)
    kern = functools.partial(_mlp_kernel, ft=1024, final_norm=final_norm)
    return pl.pallas_call(
        kern,
        grid=(T // tm,),
        in_specs=[
            pl.BlockSpec((tm, D), lambda i: (i, 0)),
            pl.BlockSpec((1, D), lambda i: (0, 0)),
            pl.BlockSpec((D, F), lambda i: (0, 0)),
            pl.BlockSpec((F, D), lambda i: (0, 0)),
            pl.BlockSpec((1, D), lambda i: (0, 0)),
        ],
        out_specs=pl.BlockSpec((tm, D), lambda i: (i, 0)),
        out_shape=jax.ShapeDtypeStruct((T, D), F32),
        compiler_params=_params("parallel"),
        name="mlp",
    )(x2, g.reshape(1, D).astype(F32), w_up, w_down, g_final.reshape(1, D).astype(F32))


def kernel(x, w_in, g_mix, da_lambda, da_subln_g, rel_bias, sg_ln_g, sg_ln_b, sg_w, sg_b, lru_conv_w, lru_conv_b, lru_w_a, lru_b_a, lru_w_i, lru_b_i, lru_lambda, b_gate, w_branch, w_out, g_mlp, w_up, w_down, g_final):
    B, S, D = x.shape
    T = B * S
    depth = w_in.shape[0]
    x2 = x.reshape(T, D).astype(F32)

    qscale = jnp.concatenate([jnp.full((D,), DA_HEAD_DIM ** -0.5 * LOG2E, F32),
                              jnp.ones((w_in.shape[2] - D,), F32)])
    bias_diag, bias_corner, cfar = _bias_tiles(rel_bias, ATT_TILE // 2)

    for l in range(depth):
        w_l = (w_in[l] * qscale[None, :]).astype(BF16)
        qkv, gates, yb, yc = _front(x2, S, g_mix[l], w_l, sg_ln_g[l], sg_ln_b[l], sg_w[l], sg_b[l],
                                    lru_conv_w[l], lru_conv_b[l], lru_w_a[l], lru_b_a[l],
                                    lru_w_i[l], lru_b_i[l], lru_lambda[l])
        ya = _attention(qkv, B, S, da_lambda[l], da_subln_g[l], bias_diag, bias_corner, cfar, l)
        x2 = _merge(ya, yb, yc, gates, b_gate[l], w_branch[l].astype(BF16),
                    w_out[l].astype(BF16), x2)
        x2 = _mlp(x2, g_mlp[l], w_up[l].astype(BF16), w_down[l].astype(BF16), g_final,
                  final_norm=(l == depth - 1))
    return x2.reshape(B, S, D).astype(x.dtype)
```
